```python
import math
import jax, jax.numpy as jnp
from jax import lax
import numpy as np

D_MODEL = 1024
BATCH = 2
SEQ = 16384
DEPTH = 4

DN_ALPHA = (2.0 * DEPTH) ** 0.25
DN_BETA = (8.0 * DEPTH) ** -0.25
LN_EPS = 1e-5
RMS_EPS = 1e-6
ROPE_THETA = 10000.0
QBLK = 128
ADA_SCALE = 0.2
MLA_HEADS = 8
MLA_NOPE = 64
MLA_ROPE = 32
MLA_V = 64
MLA_Q_RANK = 3 * D_MODEL // 8
MLA_KV_RANK = D_MODEL // 4
SWA_HEADS = 8
SWA_KV_HEADS = 2
SWA_HD = 64
SWA_WINDOW = 128
RET_HEADS = 4
RET_DK = 128
RET_DV = 128
RET_CHUNK = 128
S5_WIDTH = D_MODEL // 2
S5_GROUP = 16
S5_GROUPS = S5_WIDTH // S5_GROUP
S5_STATE = 64
D_FF = 4 * D_MODEL
N_EVEN = (DEPTH + 1) // 2
N_ODD = DEPTH // 2
ATT_IN = MLA_Q_RANK + MLA_KV_RANK + MLA_ROPE + SWA_HEADS * SWA_HD + 2 * SWA_KV_HEADS * SWA_HD
ATT_OUT = MLA_HEADS * MLA_V + SWA_HEADS * SWA_HD
REC_IN = RET_HEADS * (2 * RET_DK + 2 * RET_DV) + S5_WIDTH
REC_OUT = RET_HEADS * RET_DV + S5_WIDTH

kernel_name = 'hybrid_mla_swa_retention_s5_deepnorm_adaln'


def layer_norm(x, g, b):
    xf = x.astype(jnp.float32)
    mu = jnp.mean(xf, axis=-1, keepdims=True)
    var = jnp.mean(jnp.square(xf - mu), axis=-1, keepdims=True)
    return ((xf - mu) * lax.rsqrt(var + LN_EPS)).astype(x.dtype) * g + b


def rms_norm(x, g):
    xf = x.astype(jnp.float32)
    return (xf * lax.rsqrt(jnp.mean(xf * xf, axis=-1, keepdims=True) + RMS_EPS)).astype(x.dtype) * g


def rope_tables(seq, dim):
    inv = ROPE_THETA ** (-jnp.arange(0, dim, 2, dtype=jnp.float32) / dim)
    ang = jnp.arange(seq, dtype=jnp.float32)[:, None] * inv[None, :]
    return jnp.cos(ang), jnp.sin(ang)


def apply_rope(x, cos, sin):
    x1, x2 = jnp.split(x, 2, axis=-1)
    out = jnp.concatenate([x1 * cos - x2 * sin, x2 * cos + x1 * sin], axis=-1)
    return out.astype(x.dtype)


def adaln(cond, w, b):
    mod = (cond @ w + b)[:, None, :]
    return jnp.split(mod, 3, axis=-1)


def mla_attention(c_q, c_kv, k_rope_in, q_norm, w_uq, kv_norm, w_ukv, cos, sin):
    B, S, _ = c_q.shape
    H = MLA_HEADS
    q = (rms_norm(c_q, q_norm) @ w_uq).reshape(B, S, H, MLA_NOPE + MLA_ROPE)
    q_nope = q[..., :MLA_NOPE]
    q_rope = apply_rope(q[..., MLA_NOPE:], cos[:, None], sin[:, None])
    kv = (rms_norm(c_kv, kv_norm) @ w_ukv).reshape(B, S, H, MLA_NOPE + MLA_V)
    k_nope, v = kv[..., :MLA_NOPE], kv[..., MLA_NOPE:]
    k_rope = apply_rope(k_rope_in, cos, sin)
    scale = (MLA_NOPE + MLA_ROPE) ** -0.5
    nq = S // QBLK
    qn_blk = q_nope.reshape(B, nq, QBLK, H, MLA_NOPE).transpose(1, 0, 2, 3, 4)
    qr_blk = q_rope.reshape(B, nq, QBLK, H, MLA_ROPE).transpose(1, 0, 2, 3, 4)
    kpos = jnp.arange(S)

    def block(args):
        qn, qr, i = args
        s = jnp.einsum('bqhd,bkhd->bhqk', qn, k_nope) + jnp.einsum('bqhr,bkr->bhqk', qr, k_rope)
        s = s.astype(jnp.float32) * scale
        qpos = i * QBLK + jnp.arange(QBLK)
        s = jnp.where(kpos[None, :] <= qpos[:, None], s, -jnp.inf)
        p = jax.nn.softmax(s, axis=-1).astype(v.dtype)
        return jnp.einsum('bhqk,bkhe->bqhe', p, v)

    o = lax.map(block, (qn_blk, qr_blk, jnp.arange(nq)))
    return o.transpose(1, 0, 2, 3, 4).reshape(B, S, H * MLA_V)


def swa_attention(q, k, v, sinks, cos, sin):
    B, S, _ = q.shape
    KV, HD, W = SWA_KV_HEADS, SWA_HD, SWA_WINDOW
    G = SWA_HEADS // KV
    q = apply_rope(q.reshape(B, S, KV, G, HD), cos[:, None, None], sin[:, None, None])
    k = apply_rope(k.reshape(B, S, KV, HD), cos[:, None], sin[:, None])
    v = v.reshape(B, S, KV, HD)
    nb = S // W
    qb = q.reshape(B, nb, W, KV, G, HD)
    kb = k.reshape(B, nb, W, KV, HD)
    vb = v.reshape(B, nb, W, KV, HD)
    shift = lambda t: jnp.concatenate([jnp.zeros_like(t[:, :1]), t[:, :-1]], axis=1)
    kc = jnp.concatenate([shift(kb), kb], axis=2)
    vc = jnp.concatenate([shift(vb), vb], axis=2)
    s = jnp.einsum('bnqhgd,bnshd->bnhgqs', qb, kc).astype(jnp.float32) * HD ** -0.5
    i = jnp.arange(W)[:, None]
    j = jnp.arange(2 * W)[None, :]
    band = (j > i) & (j <= i + W)
    blk = jnp.arange(nb)[:, None, None]
    valid = band[None] & ((blk > 0) | (j[None] >= W))
    s = jnp.where(valid[None, :, None, None], s, -jnp.inf)
    sink = sinks.astype(jnp.float32).reshape(KV, G)[None, None, :, :, None, None]
    m = jnp.maximum(jnp.max(s, axis=-1, keepdims=True), sink)
    p = jnp.exp(s - m)
    p = (p / (jnp.sum(p, axis=-1, keepdims=True) + jnp.exp(sink - m))).astype(v.dtype)
    o = jnp.einsum('bnhgqs,bnshd->bnqhgd', p, vc)
    return o.reshape(B, S, SWA_HEADS * HD)


def retention(q, k, v, g, cos, sin):
    B, S, _ = q.shape
    H, DK, DV, C = RET_HEADS, RET_DK, RET_DV, RET_CHUNK
    q = apply_rope(q.reshape(B, S, H, DK), cos[:, None], sin[:, None]).astype(jnp.float32)
    k = apply_rope(k.reshape(B, S, H, DK), cos[:, None], sin[:, None]).astype(jnp.float32) * DK ** -0.5
    v = v.reshape(B, S, H, DV).astype(jnp.float32)
    nc = S // C
    log_gamma = jnp.log(1.0 - 2.0 ** (-5.0 - jnp.arange(H, dtype=jnp.float32)))
    idx = jnp.arange(C, dtype=jnp.float32)
    diff = idx[:, None] - idx[None, :]
    decay = jnp.where(diff >= 0, jnp.exp(log_gamma[:, None, None] * jnp.maximum(diff, 0.0)), 0.0)
    qc = q.reshape(B, nc, C, H, DK)
    kc = k.reshape(B, nc, C, H, DK)
    vc = v.reshape(B, nc, C, H, DV)
    sc = jnp.einsum('bnihd,bnjhd->bnhij', qc, kc) * decay[None, None]
    intra = jnp.einsum('bnhij,bnjhe->bnihe', sc, vc)
    to_end = jnp.exp(log_gamma[:, None] * (C - 1.0 - idx)[None, :])
    kd = kc * to_end.T[None, None, :, :, None]
    U = jnp.einsum('bnjhd,bnjhe->nbhde', kd, vc)
    chunk_decay = jnp.exp(log_gamma * C)[None, :, None, None]

    def step(state, u):
        return chunk_decay * state + u, state

    _, s_prev = lax.scan(step, jnp.zeros((B, H, DK, DV), jnp.float32), U)
    from_start = jnp.exp((idx + 1.0)[:, None] * log_gamma[None, :])
    cross = jnp.einsum('bnihd,nbhde->bnihe', qc, s_prev) * from_start[None, None, :, :, None]
    o = (intra + cross).reshape(B, S, H, DV)
    mu = jnp.mean(o, axis=-1, keepdims=True)
    var = jnp.mean(jnp.square(o - mu), axis=-1, keepdims=True)
    o = ((o - mu) * lax.rsqrt(var + LN_EPS)).reshape(B, S, H * DV)
    return (jax.nn.silu(g.astype(jnp.float32)) * o).astype(g.dtype)


def s5_layer(u, a_re, a_im, log_step, b_re, b_im, c_re, c_im, d, glu_w, glu_b):
    B, S, W = u.shape
    G, P, N = S5_GROUPS, S5_GROUP, S5_STATE
    uf = u.astype(jnp.float32).reshape(B, S, G, P)
    dt = jnp.exp(log_step.astype(jnp.float32))[:, None]
    lr, li = a_re.astype(jnp.float32), a_im.astype(jnp.float32)
    mag = jnp.exp(lr * dt)
    ar, ai = mag * jnp.cos(li * dt), mag * jnp.sin(li * dt)
    den = lr * lr + li * li
    cr = ((ar - 1.0) * lr + ai * li) / den
    ci = (ai * lr - (ar - 1.0) * li) / den
    br, bi = b_re.astype(jnp.float32), b_im.astype(jnp.float32)
    bbar_r = cr[..., None] * br - ci[..., None] * bi
    bbar_i = cr[..., None] * bi + ci[..., None] * br
    bu_r = jnp.einsum('bsgp,gnp->bsgn', uf, bbar_r)
    bu_i = jnp.einsum('bsgp,gnp->bsgn', uf, bbar_i)
    a_r = jnp.broadcast_to(ar, (1, S, G, N))
    a_i = jnp.broadcast_to(ai, (1, S, G, N))

    def combine(e1, e2):
        a1r, a1i, b1r, b1i = e1
        a2r, a2i, b2r, b2i = e2
        return (a2r * a1r - a2i * a1i, a2r * a1i + a2i * a1r,
                a2r * b1r - a2i * b1i + b2r, a2r * b1i + a2i * b1r + b2i)

    _, _, xr, xi = lax.associative_scan(combine, (a_r, a_i, bu_r, bu_i), axis=1)
    y = (jnp.einsum('bsgn,gpn->bsgp', xr, c_re.astype(jnp.float32))
         - jnp.einsum('bsgn,gpn->bsgp', xi, c_im.astype(jnp.float32)))
    y = (y.reshape(B, S, W) + d * uf.reshape(B, S, W)).astype(u.dtype)
    y = jax.nn.gelu(y)
    return y * jax.nn.sigmoid(y @ glu_w + glu_b)


def attention_mixer(h, w_in, q_norm, w_uq, kv_norm, w_ukv, sinks, w_out, rope_mla, rope_swa):
    z = h @ w_in
    sizes = [MLA_Q_RANK, MLA_KV_RANK, MLA_ROPE, SWA_HEADS * SWA_HD, SWA_KV_HEADS * SWA_HD]
    c_q, c_kv, k_rope, sq, sk, sv = jnp.split(z, np.cumsum(sizes).tolist(), axis=-1)
    o_a = mla_attention(c_q, c_kv, k_rope, q_norm, w_uq, kv_norm, w_ukv, *rope_mla)
    o_b = swa_attention(sq, sk, sv, sinks, *rope_swa)
    return jnp.concatenate([o_a, o_b], axis=-1) @ w_out


def recurrent_mixer(h, w_in, a_re, a_im, log_step, b_re, b_im, c_re, c_im, d, glu_w, glu_b, w_out, rope_ret):
    z = h @ w_in
    sizes = [RET_HEADS * RET_DK, RET_HEADS * RET_DK, RET_HEADS * RET_DV, RET_HEADS * RET_DV]
    rq, rk, rv, rg, u = jnp.split(z, np.cumsum(sizes).tolist(), axis=-1)
    o_c = retention(rq, rk, rv, rg, *rope_ret)
    o_d = s5_layer(u, a_re, a_im, log_step, b_re, b_im, c_re, c_im, d, glu_w, glu_b)
    return jnp.concatenate([o_c, o_d], axis=-1) @ w_out


def setup_inputs(seed: int = 0) -> dict:
    key = jax.random.key(seed)
    ks = iter(jax.random.split(key, 32))
    f32 = jnp.float32
    nrm = lambda shape, scale: jax.random.normal(next(ks), shape, f32) * scale
    G, P, N = S5_GROUPS, S5_GROUP, S5_STATE
    return {
        'x': nrm((BATCH, SEQ, D_MODEL), 1.0),
        'c': nrm((BATCH, D_MODEL), 1.0),
        'ada_w': nrm((DEPTH, 2, D_MODEL, 3 * D_MODEL), ADA_SCALE * D_MODEL ** -0.5),
        'ada_b': nrm((DEPTH, 2, 3 * D_MODEL), 0.01),
        'ln_g': 1.0 + nrm((DEPTH, 2, D_MODEL), 0.02),
        'ln_b': nrm((DEPTH, 2, D_MODEL), 0.02),
        'att_w_in': nrm((N_EVEN, D_MODEL, ATT_IN), D_MODEL ** -0.5),
        'mla_q_norm': 1.0 + nrm((N_EVEN, MLA_Q_RANK), 0.02),
        'mla_w_uq': nrm((N_EVEN, MLA_Q_RANK, MLA_HEADS * (MLA_NOPE + MLA_ROPE)), MLA_Q_RANK ** -0.5),
        'mla_kv_norm': 1.0 + nrm((N_EVEN, MLA_KV_RANK), 0.02),
        'mla_w_ukv': nrm((N_EVEN, MLA_KV_RANK, MLA_HEADS * (MLA_NOPE + MLA_V)), MLA_KV_RANK ** -0.5),
        'swa_sinks': nrm((N_EVEN, SWA_HEADS), 0.5),
        'att_w_out': nrm((N_EVEN, ATT_OUT, D_MODEL), DN_BETA * ATT_OUT ** -0.5),
        'rec_w_in': nrm((N_ODD, D_MODEL, REC_IN), D_MODEL ** -0.5),
        's5_a_re': -0.5 + nrm((N_ODD, G, N), 0.01),
        's5_a_im': jnp.pi * jnp.arange(N, dtype=f32)[None, None, :] + nrm((N_ODD, G, N), 0.01),
        's5_log_step': jax.random.uniform(next(ks), (N_ODD, G), f32, math.log(1e-3), math.log(1e-1)),
        's5_b_re': nrm((N_ODD, G, N, P), (2.0 * P) ** -0.5),
        's5_b_im': nrm((N_ODD, G, N, P), (2.0 * P) ** -0.5),
        's5_c_re': nrm((N_ODD, G, P, N), N ** -0.5),
        's5_c_im': nrm((N_ODD, G, P, N), N ** -0.5),
        's5_d': nrm((N_ODD, S5_WIDTH), 1.0),
        's5_glu_w': nrm((N_ODD, S5_WIDTH, S5_WIDTH), S5_WIDTH ** -0.5),
        's5_glu_b': nrm((N_ODD, S5_WIDTH), 0.01),
        'rec_w_out': nrm((N_ODD, REC_OUT, D_MODEL), DN_BETA * REC_OUT ** -0.5),
        'mlp_w1': nrm((DEPTH, D_MODEL, D_FF), D_MODEL ** -0.5),
        'mlp_w2': nrm((DEPTH, D_FF, D_MODEL), DN_BETA * D_FF ** -0.5),
    }


def reference(x, c, ada_w, ada_b, ln_g, ln_b, att_w_in, mla_q_norm, mla_w_uq, mla_kv_norm, mla_w_ukv,
              swa_sinks, att_w_out, rec_w_in, s5_a_re, s5_a_im, s5_log_step, s5_b_re, s5_b_im,
              s5_c_re, s5_c_im, s5_d, s5_glu_w, s5_glu_b, rec_w_out, mlp_w1, mlp_w2):
    S = x.shape[1]
    cond = jax.nn.silu(c)
    rope_mla = rope_tables(S, MLA_ROPE)
    rope_swa = rope_tables(S, SWA_HD)
    rope_ret = rope_tables(S, RET_DK)
    for l in range(DEPTH):
        j = l // 2
        shift, scale, gate = adaln(cond, ada_w[l, 0], ada_b[l, 0])
        h = x * (1.0 + scale) + shift
        if l % 2 == 0:
            y = attention_mixer(h, att_w_in[j], mla_q_norm[j], mla_w_uq[j], mla_kv_norm[j], mla_w_ukv[j],
                                swa_sinks[j], att_w_out[j], rope_mla, rope_swa)
        else:
            y = recurrent_mixer(h, rec_w_in[j], s5_a_re[j], s5_a_im[j], s5_log_step[j], s5_b_re[j],
                                s5_b_im[j], s5_c_re[j], s5_c_im[j], s5_d[j], s5_glu_w[j], s5_glu_b[j],
                                rec_w_out[j], rope_ret)
        x = layer_norm(DN_ALPHA * x + (1.0 + gate) * y, ln_g[l, 0], ln_b[l, 0])
        shift, scale, gate = adaln(cond, ada_w[l, 1], ada_b[l, 1])
        h = x * (1.0 + scale) + shift
        y = jnp.square(jax.nn.relu(h @ mlp_w1[l])) @ mlp_w2[l]
        x = layer_norm(DN_ALPHA * x + (1.0 + gate) * y, ln_g[l, 1], ln_b[l, 1])
    return x
```

```python
import functools
import math

import jax
import jax.numpy as jnp
import numpy as np
from jax import lax
from jax.experimental import pallas as pl
from jax.experimental.pallas import tpu as pltpu

F32 = jnp.float32
BF16 = jnp.bfloat16

D_MODEL = 1024
DEPTH = 4
DN_ALPHA = (2.0 * DEPTH) ** 0.25
LN_EPS = 1e-5
RMS_EPS = 1e-6
ROPE_THETA = 10000.0
MLA_HEADS = 8
MLA_NOPE = 64
MLA_ROPE = 32
MLA_V = 64
MLA_Q_RANK = 384
MLA_KV_RANK = 256
SWA_HEADS = 8
SWA_KV_HEADS = 2
SWA_HD = 64
SWA_WINDOW = 128
RET_HEADS = 4
RET_DK = 128
RET_DV = 128
RET_CHUNK = 128
S5_WIDTH = 512
S5_GROUP = 16
S5_GROUPS = 32
S5_STATE = 64
S5_CHUNK = 32
D_FF = 4096
FF_CHUNK = 1024

LANES = 128
VMEM_LIMIT = 56 * 1024 * 1024
NEG_BIG = -1e30
HI = lax.Precision.HIGHEST


def _cparams(sem):
    return pltpu.CompilerParams(dimension_semantics=sem, vmem_limit_bytes=VMEM_LIMIT)


def _const_spec(shape):
    nd = len(shape)
    return pl.BlockSpec(shape, lambda *_: (0,) * nd, pipeline_mode=pl.Buffered(1))


def _dot(a, b):
    return jnp.dot(a, b, preferred_element_type=F32)


def _dot_nt(a, b):
    return lax.dot_general(a, b, (((1,), (1,)), ((), ())), preferred_element_type=F32)


def _dot_tn(a, b):
    return lax.dot_general(a, b, (((0,), (0,)), ((), ())), preferred_element_type=F32)


def _rope(x, c, s):
    return x * c + pltpu.roll(x, 64, 1) * s


def _layer_norm(v, g, b):
    mu = jnp.mean(v, axis=-1, keepdims=True)
    d = v - mu
    var = jnp.mean(d * d, axis=-1, keepdims=True)
    return d * lax.rsqrt(var + LN_EPS) * g + b


def _mod_kernel(c_ref, w_ref, b_ref, o_ref):
    cond = jax.nn.silu(c_ref[...]).astype(BF16)
    o_ref[0] = _dot(cond, w_ref[0].astype(BF16)) + b_ref[0]


def _modulation(c, ada_w, ada_b):
    B = c.shape[0]
    nmod = ada_w.shape[0] * ada_w.shape[1]
    rows = 8
    tn = 768
    c8 = jnp.zeros((rows, D_MODEL), F32).at[:B].set(c)
    w = ada_w.reshape(nmod, D_MODEL, 3 * D_MODEL)
    b = ada_b.reshape(nmod, 1, 3 * D_MODEL)
    out = pl.pallas_call(
        _mod_kernel,
        out_shape=jax.ShapeDtypeStruct((nmod, rows, 3 * D_MODEL), F32),
        grid=(nmod, 3 * D_MODEL // tn),
        in_specs=[
            pl.BlockSpec((rows, D_MODEL), lambda m, n: (0, 0)),
            pl.BlockSpec((1, D_MODEL, tn), lambda m, n: (m, 0, n)),
            pl.BlockSpec((1, 1, tn), lambda m, n: (m, 0, n)),
        ],
        out_specs=pl.BlockSpec((1, rows, tn), lambda m, n: (m, 0, n)),
        compiler_params=_cparams(("arbitrary", "arbitrary")),
        name="adaln_mod",
    )(c8, w, b)
    return out[:, :B, :]


def _rope_tables(seq, dim, x1_start, scale):
    half = dim // 2
    inv = ROPE_THETA ** (-jnp.arange(0, dim, 2, dtype=F32) / dim)
    ang = jnp.arange(seq, dtype=F32)[:, None] * inv[None, :]
    cos, sin = jnp.cos(ang), jnp.sin(ang)
    c = jnp.ones((seq, LANES), F32)
    s = jnp.zeros((seq, LANES), F32)
    c = c.at[:, x1_start:x1_start + half].set(cos).at[:, 64 + x1_start:64 + x1_start + half].set(cos)
    s = s.at[:, x1_start:x1_start + half].set(-sin).at[:, 64 + x1_start:64 + x1_start + half].set(sin)
    return c * scale, s * scale


def _place_cols(w, src):
    src = np.asarray(src)
    cols = jnp.take(w, jnp.asarray(np.maximum(src, 0)), axis=1)
    return jnp.where(jnp.asarray(src >= 0)[None, :], cols, 0.0)


def _att_layouts():
    off_kr = MLA_Q_RANK + MLA_KV_RANK
    off_sq = off_kr + MLA_ROPE
    off_sk = off_sq + SWA_HEADS * SWA_HD
    off_sv = off_sk + SWA_KV_HEADS * SWA_HD
    src = list(range(off_kr))
    grp = [-1] * LANES
    for i in range(16):
        grp[i] = off_kr + i
        grp[64 + i] = off_kr + 16 + i
    src += grp
    for base, nh in ((off_sq, SWA_HEADS), (off_sk, SWA_KV_HEADS)):
        for h in range(nh):
            grp = [-1] * LANES
            for i in range(32):
                grp[i] = base + h * SWA_HD + i
                grp[64 + i] = base + h * SWA_HD + 32 + i
            src += grp
    src += list(range(off_sv, off_sv + SWA_KV_HEADS * SWA_HD))
    uq, uk, uv = [], [], []
    for h in range(MLA_HEADS):
        qb = h * (MLA_NOPE + MLA_ROPE)
        kb = h * (MLA_NOPE + MLA_V)
        gq = [-1] * LANES
        gk = [-1] * LANES
        for i in range(16):
            gq[i] = qb + MLA_NOPE + i
            gq[64 + i] = qb + MLA_NOPE + 16 + i
        for i in range(48):
            gq[16 + i] = qb + i
            gk[16 + i] = kb + i
        for i in range(16):
            gq[80 + i] = qb + 48 + i
            gk[80 + i] = kb + 48 + i
        uq += gq
        uk += gk
        uv += list(range(kb + MLA_NOPE, kb + MLA_NOPE + MLA_V))
    return src, uq, uk, uv


def _att_in_kernel(x_ref, mod_ref, win_ref, qn_ref, wuq_ref, kvn_ref, wuk_ref, wuv_ref, tab_ref,
                   qm_ref, km_ref, vm_ref, qs_ref, ks_ref, vs_ref):
    mod = mod_ref[0]
    h = (x_ref[0] * (1.0 + mod[1:2, :]) + mod[0:1, :]).astype(BF16)
    z = _dot(h, win_ref[...])
    tab = tab_ref[...]
    t = lambda k: tab[:, k * LANES:(k + 1) * LANES]

    c_q = z[:, :MLA_Q_RANK]
    c_q = c_q * lax.rsqrt(jnp.mean(c_q * c_q, axis=-1, keepdims=True) + RMS_EPS) * qn_ref[...]
    q = _dot(c_q.astype(BF16), wuq_ref[...])
    c_kv = z[:, MLA_Q_RANK:MLA_Q_RANK + MLA_KV_RANK]
    c_kv = c_kv * lax.rsqrt(jnp.mean(c_kv * c_kv, axis=-1, keepdims=True) + RMS_EPS) * kvn_ref[...]
    c_kv = c_kv.astype(BF16)
    kup = _dot(c_kv, wuk_ref[...])
    v = _dot(c_kv, wuv_ref[...])
    off = MLA_Q_RANK + MLA_KV_RANK
    kr = _rope(z[:, off:off + LANES], t(2), t(3))
    for hd in range(MLA_HEADS):
        sl = slice(hd * LANES, (hd + 1) * LANES)
        qm_ref[0, hd] = _rope(q[:, sl], t(0), t(1)).astype(BF16)
        km_ref[0, hd] = (kup[:, sl] + kr).astype(BF16)
    for hp in range(MLA_HEADS // 2):
        vm_ref[0, hp] = v[:, hp * LANES:(hp + 1) * LANES].astype(BF16)
    off += LANES
    for hd in range(SWA_HEADS):
        qs_ref[0, hd] = _rope(z[:, off + hd * LANES:off + (hd + 1) * LANES], t(4), t(5)).astype(BF16)
    off += SWA_HEADS * LANES
    for kv in range(SWA_KV_HEADS):
        ks_ref[0, kv] = _rope(z[:, off + kv * LANES:off + (kv + 1) * LANES], t(6), t(7)).astype(BF16)
    off += SWA_KV_HEADS * LANES
    sv = z[:, off:off + LANES]
    vs_ref[0, 0] = sv.astype(BF16)
    vs_ref[0, 1] = pltpu.roll(sv, 64, 1).astype(BF16)


def _att_in(x, mod, w_in_p, q_norm, w_uq_p, kv_norm, w_uk_p, w_uv_p, tab, tm):
    B, S, D = x.shape
    n1 = w_in_p.shape[1]
    bf = lambda *s: jax.ShapeDtypeStruct(s, BF16)
    tok4 = lambda nh: pl.BlockSpec((1, nh, tm, LANES), lambda b, i: (b, 0, i, 0))
    return pl.pallas_call(
        _att_in_kernel,
        out_shape=(bf(B, MLA_HEADS, S, LANES), bf(B, MLA_HEADS, S, LANES), bf(B, MLA_HEADS // 2, S, LANES),
                   bf(B, SWA_HEADS, S, LANES), bf(B, SWA_KV_HEADS, S, LANES), bf(B, 2, S, LANES)),
        grid=(B, S // tm),
        in_specs=[
            pl.BlockSpec((1, tm, D), lambda b, i: (b, i, 0)),
            pl.BlockSpec((1, 8, D), lambda b, i: (b, 0, 0)),
            _const_spec((D, n1)),
            _const_spec((1, MLA_Q_RANK)),
            _const_spec((MLA_Q_RANK, MLA_HEADS * LANES)),
            _const_spec((1, MLA_KV_RANK)),
            _const_spec((MLA_KV_RANK, MLA_HEADS * LANES)),
            _const_spec((MLA_KV_RANK, MLA_HEADS * MLA_V)),
            pl.BlockSpec((tm, 8 * LANES), lambda b, i: (i, 0)),
        ],
        out_specs=(tok4(MLA_HEADS), tok4(MLA_HEADS), tok4(MLA_HEADS // 2),
                   tok4(SWA_HEADS), tok4(SWA_KV_HEADS), tok4(2)),
        compiler_params=_cparams(("arbitrary", "arbitrary")),
        name="att_in_proj",
    )(x, mod, w_in_p, q_norm, w_uq_p, kv_norm, w_uk_p, w_uv_p, tab)


def _mla_kernel(q_ref, k_ref, v_ref, o_ref, m_sc, l_sc, acc_sc, *, tq, tk):
    i = pl.program_id(2)
    j = pl.program_id(3)

    @pl.when(j == 0)
    def _():
        m_sc[...] = jnp.full(m_sc.shape, NEG_BIG, F32)
        l_sc[...] = jnp.zeros(l_sc.shape, F32)
        acc_sc[...] = jnp.zeros(acc_sc.shape, F32)

    def step(masked):
        v = v_ref[0, 0]
        for hh in range(2):
            s = _dot_nt(q_ref[0, hh], k_ref[0, hh])
            if masked:
                row = lax.broadcasted_iota(jnp.int32, (tq, tk), 0)
                col = lax.broadcasted_iota(jnp.int32, (tq, tk), 1)
                s = jnp.where(col <= row, s, NEG_BIG)
            m_prev = m_sc[hh]
            m_new = jnp.maximum(m_prev, jnp.max(s, axis=-1, keepdims=True))
            alpha = jnp.exp(m_prev - m_new)
            p = jnp.exp(s - m_new)
            l_sc[hh] = alpha * l_sc[hh] + jnp.sum(p, axis=-1, keepdims=True)
            acc_sc[hh] = alpha * acc_sc[hh] + _dot(p.astype(BF16), v)
            m_sc[hh] = m_new

    @pl.when(j < i)
    def _():
        step(False)

    @pl.when(j == i)
    def _():
        step(True)
        lane = lax.broadcasted_iota(jnp.int32, (tq, LANES), 1)
        o0 = acc_sc[0] / l_sc[0]
        o1 = acc_sc[1] / l_sc[1]
        o_ref[0] = jnp.where(lane < MLA_V, o0, o1).astype(BF16)


def _mla_attention(qm, km, vm, t):
    B, H, S, _ = qm.shape
    n = S // t
    kv_idx = lambda b, hp, i, j: (b, hp, jnp.minimum(i, j), 0)
    return pl.pallas_call(
        functools.partial(_mla_kernel, tq=t, tk=t),
        out_shape=jax.ShapeDtypeStruct((B, S, H * MLA_V), BF16),
        grid=(B, H // 2, n, n),
        in_specs=[
            pl.BlockSpec((1, 2, t, LANES), lambda b, hp, i, j: (b, hp, i, 0)),
            pl.BlockSpec((1, 2, t, LANES), kv_idx),
            pl.BlockSpec((1, 1, t, LANES), kv_idx),
        ],
        out_specs=pl.BlockSpec((1, t, LANES), lambda b, hp, i, j: (b, i, hp)),
        scratch_shapes=[pltpu.VMEM((2, t, 1), F32), pltpu.VMEM((2, t, 1), F32),
                        pltpu.VMEM((2, t, LANES), F32)],
        compiler_params=_cparams(("arbitrary", "arbitrary", "arbitrary", "arbitrary")),
        name="mla_flash",
    )(qm, km, vm)


def _swa_kernel(sink_ref, q_ref, kc_ref, kp_ref, vc_ref, vp_ref, o_ref, *, tq):
    W = SWA_WINDOW
    G = SWA_HEADS // SWA_KV_HEADS
    i = pl.program_id(1)
    nsub = tq // W
    r = lax.broadcasted_iota(jnp.int32, (G * W, 2 * W), 0) % W
    c = lax.broadcasted_iota(jnp.int32, (G * W, 2 * W), 1)
    band = (c > r) & (c <= r + W)
    first = band & ((i > 0) | (c >= W))
    lane = lax.broadcasted_iota(jnp.int32, (W, LANES), 1)
    for kv in range(SWA_KV_HEADS):
        kcat = jnp.concatenate([kp_ref[0, kv], kc_ref[0, kv]], axis=0)
        va = jnp.concatenate([vp_ref[0, 0], vc_ref[0, 0]], axis=0)
        vb = jnp.concatenate([vp_ref[0, 1], vc_ref[0, 1]], axis=0)
        v_even, v_odd = (va, vb) if kv == 0 else (vb, va)
        sink = jnp.concatenate(
            [jnp.full((W, 1), sink_ref[kv * G + g], F32) for g in range(G)], axis=0)
        for n in range(nsub):
            q4 = jnp.concatenate([q_ref[0, kv * G + g, n * W:(n + 1) * W, :] for g in range(G)], axis=0)
            s = _dot_nt(q4, kcat[n * W:(n + 2) * W])
            s = jnp.where(first if n == 0 else band, s, NEG_BIG)
            m = jnp.maximum(jnp.max(s, axis=-1, keepdims=True), sink)
            p = jnp.exp(s - m)
            den = jnp.sum(p, axis=-1, keepdims=True) + jnp.exp(sink - m)
            p = (p / den).astype(BF16)
            for pr in range(G // 2):
                oe = _dot(p[(2 * pr) * W:(2 * pr + 1) * W], v_even[n * W:(n + 2) * W])
                oo = _dot(p[(2 * pr + 1) * W:(2 * pr + 2) * W], v_odd[n * W:(n + 2) * W])
                blk = kv * (G // 2) + pr
                o_ref[0, n * W:(n + 1) * W, blk * LANES:(blk + 1) * LANES] = (
                    jnp.where(lane < SWA_HD, oe, oo).astype(BF16))


def _swa_attention(sinks, qs, ks, vs, tq):
    B, _, S, _ = qs.shape
    W = SWA_WINDOW
    nb = tq // W
    cur = lambda nh: pl.BlockSpec((1, nh, tq, LANES), lambda b, i: (b, 0, i, 0))
    prev = lambda nh: pl.BlockSpec((1, nh, W, LANES), lambda b, i: (b, 0, jnp.maximum(i * nb - 1, 0), 0))
    return pl.pallas_call(
        functools.partial(_swa_kernel, tq=tq),
        out_shape=jax.ShapeDtypeStruct((B, S, SWA_HEADS * SWA_HD), BF16),
        grid=(B, S // tq),
        in_specs=[
            pl.BlockSpec(memory_space=pltpu.SMEM),
            cur(SWA_HEADS), cur(SWA_KV_HEADS), prev(SWA_KV_HEADS), cur(2), prev(2),
        ],
        out_specs=pl.BlockSpec((1, tq, SWA_HEADS * SWA_HD), lambda b, i: (b, i, 0)),
        compiler_params=_cparams(("arbitrary", "arbitrary")),
        name="swa_attn",
    )(sinks, qs, ks, ks, vs, vs)


def _mlp_tail(x, y, mod, wout_unused, w1_ref, w2_ref):
    del wout_unused
    x1 = _layer_norm(DN_ALPHA * x + (1.0 + mod[0:1, :]) * y, mod[4:5, :], mod[5:6, :])
    h = (x1 * (1.0 + mod[2:3, :]) + mod[1:2, :]).astype(BF16)
    acc = None
    for cidx in range(D_FF // FF_CHUNK):
        hid = _dot(h, w1_ref[:, cidx * FF_CHUNK:(cidx + 1) * FF_CHUNK])
        hid = jnp.square(jnp.maximum(hid, 0.0)).astype(BF16)
        part = _dot(hid, w2_ref[cidx * FF_CHUNK:(cidx + 1) * FF_CHUNK, :])
        acc = part if acc is None else acc + part
    return _layer_norm(DN_ALPHA * x1 + (1.0 + mod[3:4, :]) * acc, mod[6:7, :], mod[7:8, :])


def _att_post_kernel(x_ref, oa_ref, ob_ref, mod_ref, wout_ref, w1_ref, w2_ref, o_ref):
    half = wout_ref.shape[0] // 2
    y = _dot(oa_ref[0], wout_ref[:half, :]) + _dot(ob_ref[0], wout_ref[half:, :])
    o_ref[0] = _mlp_tail(x_ref[0], y, mod_ref[0], None, w1_ref, w2_ref)


def _rec_post_kernel(x_ref, oc_ref, ys_ref, u_ref, mod_ref, d_ref, gw_ref, gb_ref,
                     wout_ref, w1_ref, w2_ref, o_ref):
    half = wout_ref.shape[0] // 2
    yy = jax.nn.gelu(ys_ref[0] + d_ref[...] * u_ref[0])
    od = yy * jax.nn.sigmoid(_dot(yy.astype(BF16), gw_ref[...]) + gb_ref[...])
    y = _dot(oc_ref[0], wout_ref[:half, :]) + _dot(od.astype(BF16), wout_ref[half:, :])
    o_ref[0] = _mlp_tail(x_ref[0], y, mod_ref[0], None, w1_ref, w2_ref)


def _post(kernel, x, toks, mod, consts, w_out, w1, w2, tm, name):
    B, S, D = x.shape
    tok = lambda a: pl.BlockSpec((1, tm, a.shape[-1]), lambda b, i: (b, i, 0))
    return pl.pallas_call(
        kernel,
        out_shape=jax.ShapeDtypeStruct((B, S, D), F32),
        grid=(B, S // tm),
        in_specs=([tok(x)] + [tok(a) for a in toks]
                  + [pl.BlockSpec((1, 8, D), lambda b, i: (b, 0, 0))]
                  + [_const_spec(a.shape) for a in consts]
                  + [_const_spec(w_out.shape), _const_spec(w1.shape), _const_spec(w2.shape)]),
        out_specs=tok(x),
        compiler_params=_cparams(("arbitrary", "arbitrary")),
        name=name,
    )(x, *toks, mod, *consts, w_out, w1, w2)


def _rec_in_kernel(x_ref, mod_ref, win_ref, tab_ref, q_ref, k_ref, v_ref, g_ref, u_ref):
    mod = mod_ref[0]
    h = (x_ref[0] * (1.0 + mod[1:2, :]) + mod[0:1, :]).astype(BF16)
    z = _dot(h, win_ref[...])
    tab = tab_ref[...]
    t = lambda k: tab[:, k * LANES:(k + 1) * LANES]
    W = RET_HEADS * RET_DK
    for hd in range(RET_HEADS):
        sl = slice(hd * LANES, (hd + 1) * LANES)
        q_ref[0, :, sl] = _rope(z[:, hd * LANES:(hd + 1) * LANES], t(0), t(1)).astype(BF16)
        k_ref[0, :, sl] = _rope(z[:, W + hd * LANES:W + (hd + 1) * LANES], t(2), t(3))
    v_ref[0] = z[:, 2 * W:3 * W].astype(BF16)
    g_ref[0] = z[:, 3 * W:4 * W]
    u_ref[0] = z[:, 4 * W:]


def _rec_in(x, mod, w_in, tab, tm):
    B, S, D = x.shape
    W = RET_HEADS * RET_DK
    sd = lambda dt: jax.ShapeDtypeStruct((B, S, W), dt)
    tok = pl.BlockSpec((1, tm, W), lambda b, i: (b, i, 0))
    return pl.pallas_call(
        _rec_in_kernel,
        out_shape=(sd(BF16), sd(F32), sd(BF16), sd(F32), sd(F32)),
        grid=(B, S // tm),
        in_specs=[
            pl.BlockSpec((1, tm, D), lambda b, i: (b, i, 0)),
            pl.BlockSpec((1, 8, D), lambda b, i: (b, 0, 0)),
            _const_spec(w_in.shape),
            pl.BlockSpec((tm, 4 * LANES), lambda b, i: (i, 0)),
        ],
        out_specs=(tok, tok, tok, tok, tok),
        compiler_params=_cparams(("arbitrary", "arbitrary")),
        name="rec_in_proj",
    )(x, mod, w_in, tab)


def _ret_kernel(q_ref, k_ref, v_ref, g_ref, dec_ref, te_ref, fs_ref, cd_ref, o_ref, st_sc, *, tm):
    C = RET_CHUNK

    @pl.when(pl.program_id(1) == 0)
    def _():
        st_sc[...] = jnp.zeros(st_sc.shape, F32)

    for hd in range(RET_HEADS):
        sl = slice(hd * LANES, (hd + 1) * LANES)
        state = st_sc[hd]
        for n in range(tm // C):
            rows = slice(n * C, (n + 1) * C)
            q = q_ref[0, rows, sl]
            k = k_ref[0, rows, sl]
            v = v_ref[0, rows, sl]
            sc = _dot_nt(q, k.astype(BF16)) * dec_ref[hd]
            o = _dot(sc.astype(BF16), v) + _dot(q, state.astype(BF16)) * fs_ref[hd]
            upd = _dot_tn((k * te_ref[hd]).astype(BF16), v)
            state = cd_ref[hd] * state + upd
            mu = jnp.mean(o, axis=-1, keepdims=True)
            d = o - mu
            var = jnp.mean(d * d, axis=-1, keepdims=True)
            o = d * lax.rsqrt(var + LN_EPS)
            o_ref[0, rows, sl] = (jax.nn.silu(g_ref[0, rows, sl]) * o).astype(BF16)
        st_sc[hd] = state


def _retention(q, k, v, g, tm):
    B, S, W = q.shape
    H, C = RET_HEADS, RET_CHUNK
    log_gamma = jnp.log(1.0 - 2.0 ** (-5.0 - jnp.arange(H, dtype=F32)))
    idx = jnp.arange(C, dtype=F32)
    diff = idx[:, None] - idx[None, :]
    decay = jnp.where(diff >= 0, jnp.exp(log_gamma[:, None, None] * jnp.maximum(diff, 0.0)), 0.0)
    to_end = jnp.exp(log_gamma[:, None] * (C - 1.0 - idx)[None, :])
    from_start = jnp.exp((idx + 1.0)[None, :] * log_gamma[:, None])
    chunk_decay = jnp.exp(log_gamma * C)
    bc = lambda a: jnp.broadcast_to(a[:, :, None], (H, C, LANES))
    cd = jnp.broadcast_to(chunk_decay[:, None, None], (H, RET_DK, RET_DV))
    tok = pl.BlockSpec((1, tm, W), lambda b, i: (b, i, 0))
    tbl = _const_spec((H, C, LANES))
    return pl.pallas_call(
        functools.partial(_ret_kernel, tm=tm),
        out_shape=jax.ShapeDtypeStruct((B, S, W), BF16),
        grid=(B, S // tm),
        in_specs=[tok, tok, tok, tok, tbl, tbl, tbl, tbl],
        out_specs=tok,
        scratch_shapes=[pltpu.VMEM((H, RET_DK, RET_DV), F32)],
        compiler_params=_cparams(("arbitrary", "arbitrary")),
        name="retention",
    )(q, k, v, g, decay, bc(to_end), bc(from_start), cd)


def _s5_kernel(u_ref, toep_ref, we_ref, wc_ref, mc_ref, ms_ref, y_ref, *, nchunk, nsteps):
    u = u_ref[0]
    R = u.shape[0]
    y = _dot(u, toep_ref[0])
    x = _dot(u, we_ref[0])
    cidx = lax.broadcasted_iota(jnp.int32, (R, LANES), 0) % nchunk
    mc = mc_ref[0]
    ms = ms_ref[0]
    for kk in range(nsteps):
        sh = 1 << kk
        xs = jnp.where(cidx >= sh, pltpu.roll(x, sh, 0), 0.0)
        x = x + xs * mc[kk:kk + 1, :] + pltpu.roll(xs, 64, 1) * ms[kk:kk + 1, :]
    xprev = jnp.where(cidx >= 1, pltpu.roll(x, 1, 0), 0.0)
    y_ref[0] = y + _dot(xprev.astype(BF16), wc_ref[0])


def _s5_prep(a_re, a_im, log_step, b_re, b_im, c_re, c_im, L, nsteps):
    G, N, P = S5_GROUPS, S5_STATE, S5_GROUP
    dt = jnp.exp(log_step)[:, None]
    lr, li = a_re, a_im
    la, th = lr * dt, li * dt
    mag = jnp.exp(la)
    ar, ai = mag * jnp.cos(th), mag * jnp.sin(th)
    den = lr * lr + li * li
    cr = ((ar - 1.0) * lr + ai * li) / den
    ci = (ai * lr - (ar - 1.0) * li) / den
    bbr = cr[..., None] * b_re - ci[..., None] * b_im
    bbi = cr[..., None] * b_im + ci[..., None] * b_re
    tau = jnp.arange(L + 1, dtype=F32)[:, None, None]
    pm = jnp.exp(la[None] * tau)
    pr, pi = pm * jnp.cos(th[None] * tau), pm * jnp.sin(th[None] * tau)
    car = c_re[None] * pr[:, :, None, :] - c_im[None] * pi[:, :, None, :]
    cai = c_re[None] * pi[:, :, None, :] + c_im[None] * pr[:, :, None, :]
    kern = (jnp.einsum('tgpn,gnq->tgpq', car[:L], bbr, precision=HI)
            - jnp.einsum('tgpn,gnq->tgpq', cai[:L], bbi, precision=HI))
    s_i = np.arange(L)[:, None]
    t_i = np.arange(L)[None, :]
    lag = jnp.asarray(np.maximum(t_i - s_i, 0))
    toep = jnp.where(jnp.asarray(t_i >= s_i)[:, :, None, None, None], kern[lag], 0.0)
    toep = toep.transpose(2, 0, 4, 1, 3).reshape(G, L * P, L * P)
    rev_r, rev_i = pr[L - 1::-1][:L], pi[L - 1::-1][:L]
    we_r = rev_r[..., None] * bbr[None] - rev_i[..., None] * bbi[None]
    we_i = rev_r[..., None] * bbi[None] + rev_i[..., None] * bbr[None]
    we = jnp.concatenate([we_r, we_i], axis=2).transpose(1, 0, 3, 2).reshape(G, L * P, 2 * N)
    wc = jnp.concatenate([car[1:], -cai[1:]], axis=3)
    wc = wc.transpose(1, 3, 0, 2).reshape(G, 2 * N, L * P)
    zr, zi = pr[L], pi[L]
    mcs, mss = [], []
    for _ in range(nsteps):
        mcs.append(jnp.concatenate([zr, zr], axis=-1))
        mss.append(jnp.concatenate([-zi, zi], axis=-1))
        zr, zi = zr * zr - zi * zi, 2.0 * zr * zi
    mc = jnp.stack(mcs, axis=1)
    ms = jnp.stack(mss, axis=1)
    return toep.astype(BF16), we.astype(BF16), wc.astype(BF16), mc, ms


def _s5_scan(u, a_re, a_im, log_step, b_re, b_im, c_re, c_im):
    B, S, W = u.shape
    G, P, N, L = S5_GROUPS, S5_GROUP, S5_STATE, S5_CHUNK
    nchunk = S // L
    nsteps = max(1, (nchunk - 1).bit_length())
    toep, we, wc, mc, ms = _s5_prep(a_re, a_im, log_step, b_re, b_im, c_re, c_im, L, nsteps)
    R = B * nchunk
    ug = u.astype(BF16).reshape(B, nchunk, L, G, P).transpose(3, 0, 1, 2, 4).reshape(G, R, L * P)
    grp = lambda a: pl.BlockSpec((1,) + a.shape[1:], lambda g: (g, 0, 0))
    y = pl.pallas_call(
        functools.partial(_s5_kernel, nchunk=nchunk, nsteps=nsteps),
        out_shape=jax.ShapeDtypeStruct((G, R, L * P), F32),
        grid=(G,),
        in_specs=[grp(ug), grp(toep), grp(we), grp(wc), grp(mc), grp(ms)],
        out_specs=pl.BlockSpec((1, R, L * P), lambda g: (g, 0, 0)),
        compiler_params=_cparams(("arbitrary",)),
        name="s5_conv",
    )(ug, toep, we, wc, mc, ms)
    return y.reshape(G, B, nchunk, L, P).transpose(1, 2, 3, 0, 4).reshape(B, S, W)


def _pick_tile(S, pref):
    t = min(pref, S)
    assert S % t == 0, (S, t)
    return t


def kernel(x, c, ada_w, ada_b, ln_g, ln_b, att_w_in, mla_q_norm, mla_w_uq, mla_kv_norm, mla_w_ukv, swa_sinks, att_w_out, rec_w_in, s5_a_re, s5_a_im, s5_log_step, s5_b_re, s5_b_im, s5_c_re, s5_c_im, s5_d, s5_glu_w, s5_glu_b, rec_w_out, mlp_w1, mlp_w2):
    B, S, D = x.shape
    assert D == D_MODEL and S % SWA_WINDOW == 0 and S % S5_CHUNK == 0
    tm = _pick_tile(S, 512)
    t_att = _pick_tile(S, 1024)

    mods = _modulation(c, ada_w, ada_b)
    zeros = jnp.zeros((B, D), F32)

    def in_mod(m):
        rows = [m[:, :D], m[:, D:2 * D]] + [zeros] * 6
        return jnp.stack(rows, axis=1)

    def post_mod(l, m1, m2):
        bc = lambda v: jnp.broadcast_to(v[None, :], (B, D))
        rows = [m1[:, 2 * D:], m2[:, :D], m2[:, D:2 * D], m2[:, 2 * D:],
                bc(ln_g[l, 0]), bc(ln_b[l, 0]), bc(ln_g[l, 1]), bc(ln_b[l, 1])]
        return jnp.stack(rows, axis=1)

    src_in, src_uq, src_uk, src_uv = _att_layouts()
    mla_scale = (MLA_NOPE + MLA_ROPE) ** -0.5
    att_tab = jnp.concatenate(
        _rope_tables(S, MLA_ROPE, 0, mla_scale) + _rope_tables(S, MLA_ROPE, 0, 1.0)
        + _rope_tables(S, SWA_HD, 0, SWA_HD ** -0.5) + _rope_tables(S, SWA_HD, 0, 1.0), axis=1)
    rec_tab = jnp.concatenate(
        _rope_tables(S, RET_DK, 0, 1.0) + _rope_tables(S, RET_DK, 0, RET_DK ** -0.5), axis=1)

    for l in range(DEPTH):
        j = l // 2
        m1, m2 = mods[2 * l], mods[2 * l + 1]
        pmod = post_mod(l, m1, m2)
        w1 = mlp_w1[l].astype(BF16)
        w2 = mlp_w2[l].astype(BF16)
        if l % 2 == 0:
            qm, km, vm, qs, ks, vs = _att_in(
                x, in_mod(m1),
                _place_cols(att_w_in[j], src_in).astype(BF16),
                mla_q_norm[j][None, :],
                _place_cols(mla_w_uq[j], src_uq).astype(BF16),
                mla_kv_norm[j][None, :],
                _place_cols(mla_w_ukv[j], src_uk).astype(BF16),
                _place_cols(mla_w_ukv[j], src_uv).astype(BF16),
                att_tab, tm)
            o_a = _mla_attention(qm, km, vm, t_att)
            o_b = _swa_attention(swa_sinks[j], qs, ks, vs, tm)
            x = _post(_att_post_kernel, x, [o_a, o_b], pmod, [],
                      att_w_out[j].astype(BF16), w1, w2, tm, "att_post")
        else:
            rq, rk, rv, rg, u = _rec_in(x, in_mod(m1), rec_w_in[j].astype(BF16), rec_tab, tm)
            o_c = _retention(rq, rk, rv, rg, tm)
            ys = _s5_scan(u, s5_a_re[j], s5_a_im[j], s5_log_step[j], s5_b_re[j], s5_b_im[j],
                          s5_c_re[j], s5_c_im[j])
            x = _post(_rec_post_kernel, x, [o_c, ys, u], pmod,
                      [s5_d[j][None, :], s5_glu_w[j].astype(BF16), s5_glu_b[j][None, :]],
                      rec_w_out[j].astype(BF16), w1, w2, tm, "rec_post")
    return x
```

```python
import functools
import math

import jax
import jax.numpy as jnp
import numpy as np
from jax import lax
from jax.experimental import pallas as pl
from jax.experimental.pallas import tpu as pltpu

F32 = jnp.float32
BF16 = jnp.bfloat16

D_MODEL = 1024
DEPTH = 4
DN_ALPHA = (2.0 * DEPTH) ** 0.25
LN_EPS = 1e-5
RMS_EPS = 1e-6
ROPE_THETA = 10000.0
MLA_HEADS = 8
MLA_NOPE = 64
MLA_ROPE = 32
MLA_V = 64
MLA_Q_RANK = 384
MLA_KV_RANK = 256
SWA_HEADS = 8
SWA_KV_HEADS = 2
SWA_HD = 64
SWA_WINDOW = 128
RET_HEADS = 4
RET_DK = 128
RET_DV = 128
RET_CHUNK = 128
S5_WIDTH = 512
S5_GROUP = 16
S5_GROUPS = 32
S5_STATE = 64
S5_CHUNK = 32
D_FF = 4096
FF_CHUNK = 1024

LANES = 128
VMEM_LIMIT = 56 * 1024 * 1024
NEG_BIG = -1e30
HI = lax.Precision.HIGHEST
MLA_Q_SCALE = (MLA_NOPE + MLA_ROPE) ** -0.5 * math.log2(math.e)


def _cparams(sem):
    return pltpu.CompilerParams(dimension_semantics=sem, vmem_limit_bytes=VMEM_LIMIT)


def _const_spec(shape):
    nd = len(shape)
    return pl.BlockSpec(shape, lambda *_: (0,) * nd, pipeline_mode=pl.Buffered(1))


def _layer_spec(arr, lyr):
    return pl.BlockSpec((None,) + arr.shape[1:], lambda *_: (lyr, 0, 0), pipeline_mode=pl.Buffered(1))


def _dot(a, b):
    return jnp.dot(a, b, preferred_element_type=F32)


def _dot_nt(a, b):
    return lax.dot_general(a, b, (((1,), (1,)), ((), ())), preferred_element_type=F32)


def _dot_tn(a, b):
    return lax.dot_general(a, b, (((0,), (0,)), ((), ())), preferred_element_type=F32)


def _rope(x, c, s):
    return x * c + pltpu.roll(x, 64, 1) * s


def _layer_norm(v, g, b):
    mu = jnp.mean(v, axis=-1, keepdims=True)
    d = v - mu
    var = jnp.mean(d * d, axis=-1, keepdims=True)
    return d * lax.rsqrt(var + LN_EPS) * g + b


def _mod_kernel(c_ref, w_ref, b_ref, o_ref):
    cond = jax.nn.silu(c_ref[...]).astype(BF16)
    o_ref[0] = _dot(cond, w_ref[0].astype(BF16)) + b_ref[0]


def _modulation(c, ada_w, ada_b):
    B = c.shape[0]
    nmod = ada_w.shape[0] * ada_w.shape[1]
    rows = 8
    tn = 768
    c8 = jnp.zeros((rows, D_MODEL), F32).at[:B].set(c)
    w = ada_w.reshape(nmod, D_MODEL, 3 * D_MODEL)
    b = ada_b.reshape(nmod, 1, 3 * D_MODEL)
    out = pl.pallas_call(
        _mod_kernel,
        out_shape=jax.ShapeDtypeStruct((nmod, rows, 3 * D_MODEL), F32),
        grid=(nmod, 3 * D_MODEL // tn),
        in_specs=[
            pl.BlockSpec((rows, D_MODEL), lambda m, n: (0, 0)),
            pl.BlockSpec((1, D_MODEL, tn), lambda m, n: (m, 0, n)),
            pl.BlockSpec((1, 1, tn), lambda m, n: (m, 0, n)),
        ],
        out_specs=pl.BlockSpec((1, rows, tn), lambda m, n: (m, 0, n)),
        compiler_params=_cparams(("arbitrary", "arbitrary")),
        name="adaln_mod",
    )(c8, w, b)
    return out[:, :B, :]


def _rope_pieces(seq, dim):
    half = dim // 2
    inv = ROPE_THETA ** (-jnp.arange(0, dim, 2, dtype=F32) / dim)
    ang = jnp.arange(seq, dtype=F32)[:, None] * inv[None, :]
    cos, sin = jnp.cos(ang), jnp.sin(ang)
    pad = 64 - half
    one = [jnp.ones((seq, pad), F32)] if pad else []
    zero = [jnp.zeros((seq, pad), F32)] if pad else []
    return [cos] + one + [cos] + one + [-sin] + zero + [sin] + zero


def _place_cols(w, src):
    src = np.asarray(src)
    cols = jnp.take(w, jnp.asarray(np.maximum(src, 0)), axis=-1)
    return jnp.where(jnp.asarray(src >= 0), cols, 0.0)


def _att_layouts():
    off_kr = MLA_Q_RANK + MLA_KV_RANK
    off_sq = off_kr + MLA_ROPE
    off_sk = off_sq + SWA_HEADS * SWA_HD
    off_sv = off_sk + SWA_KV_HEADS * SWA_HD
    src = list(range(off_kr))
    grp = [-1] * LANES
    for i in range(16):
        grp[i] = off_kr + i
        grp[64 + i] = off_kr + 16 + i
    src += grp
    for base, nh in ((off_sq, SWA_HEADS), (off_sk, SWA_KV_HEADS)):
        for h in range(nh):
            grp = [-1] * LANES
            for i in range(32):
                grp[i] = base + h * SWA_HD + i
                grp[64 + i] = base + h * SWA_HD + 32 + i
            src += grp
    src += list(range(off_sv, off_sv + SWA_KV_HEADS * SWA_HD))
    uq, uk, uv = [], [], []
    for h in range(MLA_HEADS):
        qb = h * (MLA_NOPE + MLA_ROPE)
        kb = h * (MLA_NOPE + MLA_V)
        gq = [-1] * LANES
        gk = [-1] * LANES
        for i in range(16):
            gq[i] = qb + MLA_NOPE + i
            gq[64 + i] = qb + MLA_NOPE + 16 + i
        for i in range(48):
            gq[16 + i] = qb + i
            gk[16 + i] = kb + i
        for i in range(16):
            gq[80 + i] = qb + 48 + i
            gk[80 + i] = kb + 48 + i
        uq += gq
        uk += gk
        uv += list(range(kb + MLA_NOPE, kb + MLA_NOPE + MLA_V))
    return src, uq, uk, uv


def _att_in_kernel(x_ref, mod_ref, win_ref, qn_ref, wuq_ref, kvn_ref, wuk_ref, wuv_ref, tab_ref,
                   qm_ref, km_ref, vm_ref, qs_ref, ks_ref, vs_ref):
    mod = mod_ref[0]
    h = (x_ref[0] * (1.0 + mod[1:2, :]) + mod[0:1, :]).astype(BF16)
    z = _dot(h, win_ref[...])
    tab = tab_ref[...]
    t = lambda k: tab[:, k * LANES:(k + 1) * LANES]

    c_q = z[:, :MLA_Q_RANK]
    c_q = c_q * lax.rsqrt(jnp.mean(c_q * c_q, axis=-1, keepdims=True) + RMS_EPS) * qn_ref[...]
    q = _dot(c_q.astype(BF16), wuq_ref[...])
    c_kv = z[:, MLA_Q_RANK:MLA_Q_RANK + MLA_KV_RANK]
    c_kv = c_kv * lax.rsqrt(jnp.mean(c_kv * c_kv, axis=-1, keepdims=True) + RMS_EPS) * kvn_ref[...]
    c_kv = c_kv.astype(BF16)
    kup = _dot(c_kv, wuk_ref[...])
    v = _dot(c_kv, wuv_ref[...])
    off = MLA_Q_RANK + MLA_KV_RANK
    kr = _rope(z[:, off:off + LANES], t(0), t(1))
    for hd in range(MLA_HEADS):
        sl = slice(hd * LANES, (hd + 1) * LANES)
        qm_ref[0, hd] = (_rope(q[:, sl], t(0), t(1)) * MLA_Q_SCALE).astype(BF16)
        km_ref[0, hd] = (kup[:, sl] + kr).astype(BF16)
    for hp in range(MLA_HEADS // 2):
        vm_ref[0, hp, 0] = v[:, hp * LANES:(hp + 1) * LANES].T.astype(BF16)
    off += LANES
    for hd in range(SWA_HEADS):
        qs_ref[0, hd] = (_rope(z[:, off + hd * LANES:off + (hd + 1) * LANES], t(2), t(3))
                         * SWA_HD ** -0.5).astype(BF16)
    off += SWA_HEADS * LANES
    for kv in range(SWA_KV_HEADS):
        ks_ref[0, kv] = _rope(z[:, off + kv * LANES:off + (kv + 1) * LANES], t(2), t(3)).astype(BF16)
    off += SWA_KV_HEADS * LANES
    sv = z[:, off:off + LANES]
    vs_ref[0, 0] = sv.astype(BF16)
    vs_ref[0, 1] = pltpu.roll(sv, 64, 1).astype(BF16)


def _att_in(x, mod, lyr, w_in_p, q_norm, w_uq_p, kv_norm, w_uk_p, w_uv_p, tab, tm):
    B, S, D = x.shape
    bf = lambda *s: jax.ShapeDtypeStruct(s, BF16)
    tok4 = lambda nh: pl.BlockSpec((1, nh, tm, LANES), lambda b, i: (b, 0, i, 0))
    return pl.pallas_call(
        _att_in_kernel,
        out_shape=(bf(B, MLA_HEADS, S, LANES), bf(B, MLA_HEADS, S, LANES),
                   bf(B, MLA_HEADS // 2, S // tm, LANES, tm),
                   bf(B, SWA_HEADS, S, LANES), bf(B, SWA_KV_HEADS, S, LANES), bf(B, 2, S, LANES)),
        grid=(B, S // tm),
        in_specs=[
            pl.BlockSpec((1, tm, D), lambda b, i: (b, i, 0)),
            pl.BlockSpec((1, 8, D), lambda b, i: (b, 0, 0)),
            _layer_spec(w_in_p, lyr), _layer_spec(q_norm, lyr), _layer_spec(w_uq_p, lyr),
            _layer_spec(kv_norm, lyr), _layer_spec(w_uk_p, lyr), _layer_spec(w_uv_p, lyr),
            pl.BlockSpec((tm, tab.shape[1]), lambda b, i: (i, 0)),
        ],
        out_specs=(tok4(MLA_HEADS), tok4(MLA_HEADS),
                   pl.BlockSpec((1, MLA_HEADS // 2, 1, LANES, tm), lambda b, i: (b, 0, i, 0, 0)),
                   tok4(SWA_HEADS), tok4(SWA_KV_HEADS), tok4(2)),
        compiler_params=_cparams(("arbitrary", "arbitrary")),
        name="att_in_proj",
    )(x, mod, w_in_p, q_norm, w_uq_p, kv_norm, w_uk_p, w_uv_p, tab)


def _mla_kernel(q_ref, k_ref, vt_ref, o_ref, st0a, st0b, st1a, st1b, p0a, p0b, p1a, p1b, acc_sc, *, t, h):
    i = pl.program_id(2)
    assert t == 4 * h
    st_sets = ((st0a, st0b), (st1a, st1b))
    p_sets = ((p0a, p0b), (p1a, p1b))

    def scores(blk, st_wr):
        k0 = pl.multiple_of(blk * h, h)
        for hh in range(2):
            st_wr[hh] = _dot_nt(k_ref[0, hh, pl.ds(k0, h), :], q_ref[0, hh])

    def values(vt_blk, half, p_rd, hh):
        vt = vt_ref[0, 0, vt_blk, hh * MLA_V:(hh + 1) * MLA_V, half * h:(half + 1) * h]
        return _dot(vt, p_rd[hh])

    def accumulate(vt_blk, carry, p_rd):
        for hh in range(2):
            _, _, alpha_a, alpha_b = carry[hh]
            acc = alpha_a * acc_sc[hh] + values(vt_blk, 0, p_rd[0], hh)
            acc_sc[hh] = alpha_b * acc + values(vt_blk, 1, p_rd[1], hh)

    def softmax(carry, st_rd, p_wr, diag):
        out = []
        for hh in range(2):
            m, l, _, _ = carry[hh]
            alphas = []
            for half in range(2):
                st = st_rd[half][hh]
                if diag is not None:
                    key = lax.broadcasted_iota(jnp.int32, (h, t), 0) + (2 * diag + half) * h
                    qry = lax.broadcasted_iota(jnp.int32, (h, t), 1)
                    st = jnp.where(key <= qry, st, NEG_BIG)
                m_new = jnp.maximum(m, jnp.max(st, axis=0, keepdims=True))
                alphas.append(jnp.exp2(m - m_new))
                p = jnp.exp2(st - m_new)
                l = alphas[-1] * l + jnp.sum(p, axis=0, keepdims=True)
                p_wr[half][hh] = p.astype(BF16)
                m = m_new
            out.append((m, l, alphas[0], alphas[1]))
        return tuple(out)

    def iteration(c, par, carry, diag=None, more=True):
        if more:
            scores(2 * c + 2, st_sets[1 - par][0])
            scores(2 * c + 3, st_sets[1 - par][1])
        accumulate(jnp.maximum(c - 1, 0), carry, p_sets[1 - par])
        return softmax(carry, st_sets[par], p_sets[par], diag)

    def two_pairs(cc, carry):
        carry = iteration(2 * cc, 0, carry)
        return iteration(2 * cc + 1, 1, carry)

    scores(0, st0a)
    scores(1, st0b)
    p1a[...] = jnp.zeros(p1a.shape, BF16)
    p1b[...] = jnp.zeros(p1b.shape, BF16)
    acc_sc[...] = jnp.zeros(acc_sc.shape, F32)
    one = jnp.ones((1, t), F32)
    init = (jnp.full((1, t), NEG_BIG, F32), jnp.zeros((1, t), F32), one, one)
    carry = lax.fori_loop(0, i, two_pairs, (init, init))
    carry = iteration(2 * i, 0, carry, diag=0)
    carry = iteration(2 * i + 1, 1, carry, diag=1, more=False)
    accumulate(2 * i + 1, carry, p_sets[1])
    outs = [acc_sc[hh] / carry[hh][1] for hh in range(2)]
    o_ref[0] = jnp.concatenate(outs, axis=0).T.astype(BF16)


def _mla_attention(qm, km, vt, t):
    B, H, S, _ = qm.shape
    tv = vt.shape[-1]
    h = tv // 2
    assert t % tv == 0
    return pl.pallas_call(
        functools.partial(_mla_kernel, t=t, h=h),
        out_shape=jax.ShapeDtypeStruct((B, S, H * MLA_V), BF16),
        grid=(B, H // 2, S // t),
        in_specs=[
            pl.BlockSpec((1, 2, t, LANES), lambda b, hp, i: (b, hp, i, 0)),
            pl.BlockSpec((1, 2, S, LANES), lambda b, hp, i: (b, hp, 0, 0)),
            pl.BlockSpec((1, 1, S // tv, LANES, tv), lambda b, hp, i: (b, hp, 0, 0, 0)),
        ],
        out_specs=pl.BlockSpec((1, t, LANES), lambda b, hp, i: (b, i, hp)),
        scratch_shapes=([pltpu.VMEM((2, h, t), F32)] * 4 + [pltpu.VMEM((2, h, t), BF16)] * 4
                        + [pltpu.VMEM((2, MLA_V, t), F32)]),
        compiler_params=_cparams(("arbitrary", "arbitrary", "arbitrary")),
        name="mla_flash",
    )(qm, km, vt)


def _swa_kernel(sink_ref, q_ref, kc_ref, kp_ref, vc_ref, vp_ref, o_ref, *, tq):
    W = SWA_WINDOW
    G = SWA_HEADS // SWA_KV_HEADS
    i = pl.program_id(1)
    nsub = tq // W
    r = lax.broadcasted_iota(jnp.int32, (G * W, 2 * W), 0) % W
    c = lax.broadcasted_iota(jnp.int32, (G * W, 2 * W), 1)
    band = (c > r) & (c <= r + W)
    first = band & ((i > 0) | (c >= W))
    lane = lax.broadcasted_iota(jnp.int32, (W, LANES), 1)
    for kv in range(SWA_KV_HEADS):
        kcat = jnp.concatenate([kp_ref[0, kv], kc_ref[0, kv]], axis=0)
        va = jnp.concatenate([vp_ref[0, 0], vc_ref[0, 0]], axis=0)
        vb = jnp.concatenate([vp_ref[0, 1], vc_ref[0, 1]], axis=0)
        v_even, v_odd = (va, vb) if kv == 0 else (vb, va)
        sink = jnp.concatenate(
            [jnp.full((W, 1), sink_ref[kv * G + g], F32) for g in range(G)], axis=0)
        for n in range(nsub):
            q4 = jnp.concatenate([q_ref[0, kv * G + g, n * W:(n + 1) * W, :] for g in range(G)], axis=0)
            s = _dot_nt(q4, kcat[n * W:(n + 2) * W])
            s = jnp.where(first if n == 0 else band, s, NEG_BIG)
            m = jnp.maximum(jnp.max(s, axis=-1, keepdims=True), sink)
            p = jnp.exp(s - m)
            den = jnp.sum(p, axis=-1, keepdims=True) + jnp.exp(sink - m)
            p = (p / den).astype(BF16)
            for pr in range(G // 2):
                oe = _dot(p[(2 * pr) * W:(2 * pr + 1) * W], v_even[n * W:(n + 2) * W])
                oo = _dot(p[(2 * pr + 1) * W:(2 * pr + 2) * W], v_odd[n * W:(n + 2) * W])
                blk = kv * (G // 2) + pr
                o_ref[0, n * W:(n + 1) * W, blk * LANES:(blk + 1) * LANES] = (
                    jnp.where(lane < SWA_HD, oe, oo).astype(BF16))


def _swa_attention(sinks, qs, ks, vs, tq):
    B, _, S, _ = qs.shape
    W = SWA_WINDOW
    nb = tq // W
    cur = lambda nh: pl.BlockSpec((1, nh, tq, LANES), lambda b, i: (b, 0, i, 0))
    prev = lambda nh: pl.BlockSpec((1, nh, W, LANES), lambda b, i: (b, 0, jnp.maximum(i * nb - 1, 0), 0))
    return pl.pallas_call(
        functools.partial(_swa_kernel, tq=tq),
        out_shape=jax.ShapeDtypeStruct((B, S, SWA_HEADS * SWA_HD), BF16),
        grid=(B, S // tq),
        in_specs=[
            pl.BlockSpec(memory_space=pltpu.SMEM),
            cur(SWA_HEADS), cur(SWA_KV_HEADS), prev(SWA_KV_HEADS), cur(2), prev(2),
        ],
        out_specs=pl.BlockSpec((1, tq, SWA_HEADS * SWA_HD), lambda b, i: (b, i, 0)),
        compiler_params=_cparams(("arbitrary", "arbitrary")),
        name="swa_attn",
    )(sinks, qs, ks, ks, vs, vs)


def _mlp_tail(x, y, mod, w1_ref, w2_ref):
    x1 = _layer_norm(DN_ALPHA * x + (1.0 + mod[0:1, :]) * y, mod[4:5, :], mod[5:6, :])
    h = (x1 * (1.0 + mod[2:3, :]) + mod[1:2, :]).astype(BF16)
    acc = None
    for cidx in range(D_FF // FF_CHUNK):
        hid = _dot(h, w1_ref[:, cidx * FF_CHUNK:(cidx + 1) * FF_CHUNK])
        hid = jnp.square(jnp.maximum(hid, 0.0)).astype(BF16)
        part = _dot(hid, w2_ref[cidx * FF_CHUNK:(cidx + 1) * FF_CHUNK, :])
        acc = part if acc is None else acc + part
    return _layer_norm(DN_ALPHA * x1 + (1.0 + mod[3:4, :]) * acc, mod[6:7, :], mod[7:8, :])


def _att_post_kernel(x_ref, oa_ref, ob_ref, mod_ref, wout_ref, w1_ref, w2_ref, o_ref):
    half = wout_ref.shape[0] // 2
    y = _dot(oa_ref[0], wout_ref[:half, :]) + _dot(ob_ref[0], wout_ref[half:, :])
    o_ref[0] = _mlp_tail(x_ref[0], y, mod_ref[0], w1_ref, w2_ref)


def _rec_post_kernel(x_ref, oc_ref, ys_ref, u_ref, mod_ref, d_ref, gw_ref, gb_ref,
                     wout_ref, w1_ref, w2_ref, o_ref):
    half = wout_ref.shape[0] // 2
    yy = jax.nn.gelu(ys_ref[0] + d_ref[...] * u_ref[0])
    od = yy * jax.nn.sigmoid(_dot(yy.astype(BF16), gw_ref[...]) + gb_ref[...])
    y = _dot(oc_ref[0], wout_ref[:half, :]) + _dot(od.astype(BF16), wout_ref[half:, :])
    o_ref[0] = _mlp_tail(x_ref[0], y, mod_ref[0], w1_ref, w2_ref)


def _post(kernel, x, toks, mod, params, tm, name):
    B, S, D = x.shape
    tok = lambda a: pl.BlockSpec((1, tm, a.shape[-1]), lambda b, i: (b, i, 0))
    return pl.pallas_call(
        kernel,
        out_shape=jax.ShapeDtypeStruct((B, S, D), F32),
        grid=(B, S // tm),
        in_specs=([tok(x)] + [tok(a) for a in toks]
                  + [pl.BlockSpec((1, 8, D), lambda b, i: (b, 0, 0))]
                  + [_layer_spec(a, lyr) for a, lyr in params]),
        out_specs=tok(x),
        compiler_params=_cparams(("arbitrary", "arbitrary")),
        name=name,
    )(x, *toks, mod, *[a for a, _ in params])


def _rec_in_kernel(x_ref, mod_ref, win_ref, tab_ref, q_ref, k_ref, v_ref, g_ref, u_ref):
    mod = mod_ref[0]
    h = (x_ref[0] * (1.0 + mod[1:2, :]) + mod[0:1, :]).astype(BF16)
    z = _dot(h, win_ref[...])
    tab = tab_ref[...]
    t = lambda k: tab[:, k * LANES:(k + 1) * LANES]
    W = RET_HEADS * RET_DK
    for hd in range(RET_HEADS):
        sl = slice(hd * LANES, (hd + 1) * LANES)
        q_ref[0, :, sl] = _rope(z[:, hd * LANES:(hd + 1) * LANES], t(0), t(1)).astype(BF16)
        k_ref[0, :, sl] = _rope(z[:, W + hd * LANES:W + (hd + 1) * LANES], t(0), t(1)) * RET_DK ** -0.5
    v_ref[0] = z[:, 2 * W:3 * W].astype(BF16)
    g_ref[0] = z[:, 3 * W:4 * W]
    u_ref[0] = z[:, 4 * W:]


def _rec_in(x, mod, lyr, w_in, tab, tm):
    B, S, D = x.shape
    W = RET_HEADS * RET_DK
    sd = lambda dt: jax.ShapeDtypeStruct((B, S, W), dt)
    tok = pl.BlockSpec((1, tm, W), lambda b, i: (b, i, 0))
    return pl.pallas_call(
        _rec_in_kernel,
        out_shape=(sd(BF16), sd(F32), sd(BF16), sd(F32), sd(F32)),
        grid=(B, S // tm),
        in_specs=[
            pl.BlockSpec((1, tm, D), lambda b, i: (b, i, 0)),
            pl.BlockSpec((1, 8, D), lambda b, i: (b, 0, 0)),
            _layer_spec(w_in, lyr),
            pl.BlockSpec((tm, tab.shape[1]), lambda b, i: (i, 0)),
        ],
        out_specs=(tok, tok, tok, tok, tok),
        compiler_params=_cparams(("arbitrary", "arbitrary")),
        name="rec_in_proj",
    )(x, mod, w_in, tab)


def _ret_kernel(q_ref, k_ref, v_ref, g_ref, dec_ref, te_ref, fs_ref, cd_ref, o_ref, st_sc, *, tm):
    C = RET_CHUNK

    @pl.when(pl.program_id(1) == 0)
    def _():
        st_sc[...] = jnp.zeros(st_sc.shape, F32)

    for hd in range(RET_HEADS):
        sl = slice(hd * LANES, (hd + 1) * LANES)
        state = st_sc[hd]
        for n in range(tm // C):
            rows = slice(n * C, (n + 1) * C)
            q = q_ref[0, rows, sl]
            k = k_ref[0, rows, sl]
            v = v_ref[0, rows, sl]
            sc = _dot_nt(q, k.astype(BF16)) * dec_ref[hd]
            o = _dot(sc.astype(BF16), v) + _dot(q, state.astype(BF16)) * fs_ref[hd]
            upd = _dot_tn((k * te_ref[hd]).astype(BF16), v)
            state = cd_ref[hd] * state + upd
            mu = jnp.mean(o, axis=-1, keepdims=True)
            d = o - mu
            var = jnp.mean(d * d, axis=-1, keepdims=True)
            o = d * lax.rsqrt(var + LN_EPS)
            o_ref[0, rows, sl] = (jax.nn.silu(g_ref[0, rows, sl]) * o).astype(BF16)
        st_sc[hd] = state


def _retention(q, k, v, g, tm):
    B, S, W = q.shape
    H, C = RET_HEADS, RET_CHUNK
    log_gamma = jnp.log(1.0 - 2.0 ** (-5.0 - jnp.arange(H, dtype=F32)))
    idx = jnp.arange(C, dtype=F32)
    diff = idx[:, None] - idx[None, :]
    decay = jnp.where(diff >= 0, jnp.exp(log_gamma[:, None, None] * jnp.maximum(diff, 0.0)), 0.0)
    to_end = jnp.exp(log_gamma[:, None] * (C - 1.0 - idx)[None, :])
    from_start = jnp.exp((idx + 1.0)[None, :] * log_gamma[:, None])
    chunk_decay = jnp.exp(log_gamma * C)
    bc = lambda a: jnp.broadcast_to(a[:, :, None], (H, C, LANES))
    cd = jnp.broadcast_to(chunk_decay[:, None, None], (H, RET_DK, RET_DV))
    tok = pl.BlockSpec((1, tm, W), lambda b, i: (b, i, 0))
    tbl = _const_spec((H, C, LANES))
    return pl.pallas_call(
        functools.partial(_ret_kernel, tm=tm),
        out_shape=jax.ShapeDtypeStruct((B, S, W), BF16),
        grid=(B, S // tm),
        in_specs=[tok, tok, tok, tok, tbl, tbl, tbl, tbl],
        out_specs=tok,
        scratch_shapes=[pltpu.VMEM((H, RET_DK, RET_DV), F32)],
        compiler_params=_cparams(("arbitrary", "arbitrary")),
        name="retention",
    )(q, k, v, g, decay, bc(to_end), bc(from_start), cd)


def _s5_kernel(u_ref, toep_ref, we_ref, wc_ref, mc_ref, ms_ref, y_ref, *, nchunk, nsteps):
    u = u_ref[0]
    R = u.shape[0]
    y = _dot(u, toep_ref[0])
    x = _dot(u, we_ref[0])
    cidx = lax.broadcasted_iota(jnp.int32, (R, LANES), 0) % nchunk
    mc = mc_ref[0]
    ms = ms_ref[0]
    for kk in range(nsteps):
        sh = 1 << kk
        xs = jnp.where(cidx >= sh, pltpu.roll(x, sh, 0), 0.0)
        x = x + xs * mc[kk:kk + 1, :] + pltpu.roll(xs, 64, 1) * ms[kk:kk + 1, :]
    xprev = jnp.where(cidx >= 1, pltpu.roll(x, 1, 0), 0.0)
    y_ref[0] = y + _dot(xprev.astype(BF16), wc_ref[0])


def _s5_prep(a_re, a_im, log_step, b_re, b_im, c_re, c_im, L, nsteps):
    G, N, P = S5_GROUPS, S5_STATE, S5_GROUP
    dt = jnp.exp(log_step)[:, None]
    lr, li = a_re, a_im
    la, th = lr * dt, li * dt
    mag = jnp.exp(la)
    ar, ai = mag * jnp.cos(th), mag * jnp.sin(th)
    den = lr * lr + li * li
    cr = ((ar - 1.0) * lr + ai * li) / den
    ci = (ai * lr - (ar - 1.0) * li) / den
    bbr = cr[..., None] * b_re - ci[..., None] * b_im
    bbi = cr[..., None] * b_im + ci[..., None] * b_re
    tau = jnp.arange(L + 1, dtype=F32)[:, None, None]
    pm = jnp.exp(la[None] * tau)
    pr, pi = pm * jnp.cos(th[None] * tau), pm * jnp.sin(th[None] * tau)
    car = c_re[None] * pr[:, :, None, :] - c_im[None] * pi[:, :, None, :]
    cai = c_re[None] * pi[:, :, None, :] + c_im[None] * pr[:, :, None, :]
    kern = (jnp.einsum('tgpn,gnq->tgpq', car[:L], bbr, precision=HI)
            - jnp.einsum('tgpn,gnq->tgpq', cai[:L], bbi, precision=HI))
    s_i = np.arange(L)[:, None]
    t_i = np.arange(L)[None, :]
    lag = jnp.asarray(np.maximum(t_i - s_i, 0))
    toep = jnp.where(jnp.asarray(t_i >= s_i)[:, :, None, None, None], kern[lag], 0.0)
    toep = toep.transpose(2, 0, 4, 1, 3).reshape(G, L * P, L * P)
    rev_r, rev_i = pr[L - 1::-1][:L], pi[L - 1::-1][:L]
    we_r = rev_r[..., None] * bbr[None] - rev_i[..., None] * bbi[None]
    we_i = rev_r[..., None] * bbi[None] + rev_i[..., None] * bbr[None]
    we = jnp.concatenate([we_r, we_i], axis=2).transpose(1, 0, 3, 2).reshape(G, L * P, 2 * N)
    wc = jnp.concatenate([car[1:], -cai[1:]], axis=3)
    wc = wc.transpose(1, 3, 0, 2).reshape(G, 2 * N, L * P)
    zr, zi = pr[L], pi[L]
    mcs, mss = [], []
    for _ in range(nsteps):
        mcs.append(jnp.concatenate([zr, zr], axis=-1))
        mss.append(jnp.concatenate([-zi, zi], axis=-1))
        zr, zi = zr * zr - zi * zi, 2.0 * zr * zi
    mc = jnp.stack(mcs, axis=1)
    ms = jnp.stack(mss, axis=1)
    return toep.astype(BF16), we.astype(BF16), wc.astype(BF16), mc, ms


def _s5_scan(u, a_re, a_im, log_step, b_re, b_im, c_re, c_im):
    B, S, W = u.shape
    G, P, N, L = S5_GROUPS, S5_GROUP, S5_STATE, S5_CHUNK
    nchunk = S // L
    nsteps = max(1, (nchunk - 1).bit_length())
    toep, we, wc, mc, ms = _s5_prep(a_re, a_im, log_step, b_re, b_im, c_re, c_im, L, nsteps)
    R = B * nchunk
    ug = u.astype(BF16).reshape(B, nchunk, L, G, P).transpose(3, 0, 1, 2, 4).reshape(G, R, L * P)
    grp = lambda a: pl.BlockSpec((1,) + a.shape[1:], lambda g: (g, 0, 0))
    y = pl.pallas_call(
        functools.partial(_s5_kernel, nchunk=nchunk, nsteps=nsteps),
        out_shape=jax.ShapeDtypeStruct((G, R, L * P), F32),
        grid=(G,),
        in_specs=[grp(ug), grp(toep), grp(we), grp(wc), grp(mc), grp(ms)],
        out_specs=pl.BlockSpec((1, R, L * P), lambda g: (g, 0, 0)),
        compiler_params=_cparams(("arbitrary",)),
        name="s5_conv",
    )(ug, toep, we, wc, mc, ms)
    return y.reshape(G, B, nchunk, L, P).transpose(1, 2, 3, 0, 4).reshape(B, S, W)


def _pick_tile(S, pref):
    t = min(pref, S)
    assert S % t == 0, (S, t)
    return t


def kernel(x, c, ada_w, ada_b, ln_g, ln_b, att_w_in, mla_q_norm, mla_w_uq, mla_kv_norm, mla_w_ukv, swa_sinks, att_w_out, rec_w_in, s5_a_re, s5_a_im, s5_log_step, s5_b_re, s5_b_im, s5_c_re, s5_c_im, s5_d, s5_glu_w, s5_glu_b, rec_w_out, mlp_w1, mlp_w2):
    B, S, D = x.shape
    assert D == D_MODEL and S % SWA_WINDOW == 0 and S % S5_CHUNK == 0
    tm = _pick_tile(S, 512)

    mods = _modulation(c, ada_w, ada_b)
    zeros = jnp.zeros((B, D), F32)

    def in_mod(m):
        rows = [m[:, :D], m[:, D:2 * D]] + [zeros] * 6
        return jnp.stack(rows, axis=1)

    def post_mod(l, m1, m2):
        bc = lambda v: jnp.broadcast_to(v[None, :], (B, D))
        rows = [m1[:, 2 * D:], m2[:, :D], m2[:, D:2 * D], m2[:, 2 * D:],
                bc(ln_g[l, 0]), bc(ln_b[l, 0]), bc(ln_g[l, 1]), bc(ln_b[l, 1])]
        return jnp.stack(rows, axis=1)

    src_in, src_uq, src_uk, src_uv = _att_layouts()
    att_tab = jnp.concatenate(_rope_pieces(S, MLA_ROPE) + _rope_pieces(S, SWA_HD), axis=1)
    rec_tab = jnp.concatenate(_rope_pieces(S, RET_DK), axis=1)

    w_in_att = _place_cols(att_w_in, src_in).astype(BF16)
    w_uq = _place_cols(mla_w_uq, src_uq).astype(BF16)
    w_uk = _place_cols(mla_w_ukv, src_uk).astype(BF16)
    w_uv = _place_cols(mla_w_ukv, src_uv).astype(BF16)
    q_norm = mla_q_norm[:, None, :]
    kv_norm = mla_kv_norm[:, None, :]
    w_out_att = att_w_out.astype(BF16)
    w_in_rec = rec_w_in.astype(BF16)
    w_out_rec = rec_w_out.astype(BF16)
    glu_w = s5_glu_w.astype(BF16)
    s5_dd = s5_d[:, None, :]
    glu_b = s5_glu_b[:, None, :]
    w1 = mlp_w1.astype(BF16)
    w2 = mlp_w2.astype(BF16)

    for l in range(DEPTH):
        j = l // 2
        m1, m2 = mods[2 * l], mods[2 * l + 1]
        pmod = post_mod(l, m1, m2)
        if l % 2 == 0:
            qm, km, vt, qs, ks, vs = _att_in(x, in_mod(m1), j, w_in_att, q_norm, w_uq, kv_norm, w_uk, w_uv,
                                             att_tab, tm)
            o_a = _mla_attention(qm, km, vt, _pick_tile(S, 1024))
            o_b = _swa_attention(swa_sinks[j], qs, ks, vs, tm)
            x = _post(_att_post_kernel, x, [o_a, o_b], pmod,
                      [(w_out_att, j), (w1, l), (w2, l)], tm, "att_post")
        else:
            rq, rk, rv, rg, u = _rec_in(x, in_mod(m1), j, w_in_rec, rec_tab, tm)
            o_c = _retention(rq, rk, rv, rg, tm)
            ys = _s5_scan(u, s5_a_re[j], s5_a_im[j], s5_log_step[j], s5_b_re[j], s5_b_im[j],
                          s5_c_re[j], s5_c_im[j])
            x = _post(_rec_post_kernel, x, [o_c, ys, u], pmod,
                      [(s5_dd, j), (glu_w, j), (glu_b, j), (w_out_rec, j), (w1, l), (w2, l)], tm, "rec_post")
    return x
```

```python
import functools
import math

import jax
import jax.numpy as jnp
import numpy as np
from jax import lax
from jax.experimental import pallas as pl
from jax.experimental.pallas import tpu as pltpu

F32 = jnp.float32
BF16 = jnp.bfloat16

D_MODEL = 1024
DEPTH = 4
DN_ALPHA = (2.0 * DEPTH) ** 0.25
LN_EPS = 1e-5
RMS_EPS = 1e-6
ROPE_THETA = 10000.0
MLA_HEADS = 8
MLA_NOPE = 64
MLA_ROPE = 32
MLA_V = 64
MLA_Q_RANK = 384
MLA_KV_RANK = 256
SWA_HEADS = 8
SWA_KV_HEADS = 2
SWA_HD = 64
SWA_WINDOW = 128
RET_HEADS = 4
RET_DK = 128
RET_DV = 128
RET_CHUNK = 128
S5_WIDTH = 512
S5_GROUP = 16
S5_GROUPS = 32
S5_STATE = 64
S5_CHUNK = 128
D_FF = 4096
FF_CHUNK = 1024

LANES = 128
VMEM_LIMIT = 56 * 1024 * 1024
NEG_BIG = -1e30
HI = lax.Precision.HIGHEST
MLA_Q_SCALE = (MLA_NOPE + MLA_ROPE) ** -0.5 * math.log2(math.e)


def _cparams(sem):
    return pltpu.CompilerParams(dimension_semantics=sem, vmem_limit_bytes=VMEM_LIMIT)


def _const_spec(shape):
    nd = len(shape)
    return pl.BlockSpec(shape, lambda *_: (0,) * nd, pipeline_mode=pl.Buffered(1))


def _layer_spec(arr, lyr):
    return pl.BlockSpec((None,) + arr.shape[1:], lambda *_: (lyr, 0, 0), pipeline_mode=pl.Buffered(1))


def _dot(a, b):
    return jnp.dot(a, b, preferred_element_type=F32)


def _dot_nt(a, b):
    return lax.dot_general(a, b, (((1,), (1,)), ((), ())), preferred_element_type=F32)


def _dot_tn(a, b):
    return lax.dot_general(a, b, (((0,), (0,)), ((), ())), preferred_element_type=F32)


def _rope(x, c, s):
    return x * c + pltpu.roll(x, 64, 1) * s


def _layer_norm(v, g, b):
    mu = jnp.mean(v, axis=-1, keepdims=True)
    d = v - mu
    var = jnp.mean(d * d, axis=-1, keepdims=True)
    return d * lax.rsqrt(var + LN_EPS) * g + b


def _mod_kernel(c_ref, w_ref, b_ref, o_ref):
    cond = jax.nn.silu(c_ref[...]).astype(BF16)
    o_ref[0] = _dot(cond, w_ref[0].astype(BF16)) + b_ref[0]


def _modulation(c, ada_w, ada_b):
    B = c.shape[0]
    nmod = ada_w.shape[0] * ada_w.shape[1]
    rows = 8
    tn = 768
    c8 = jnp.zeros((rows, D_MODEL), F32).at[:B].set(c)
    w = ada_w.reshape(nmod, D_MODEL, 3 * D_MODEL)
    b = ada_b.reshape(nmod, 1, 3 * D_MODEL)
    out = pl.pallas_call(
        _mod_kernel,
        out_shape=jax.ShapeDtypeStruct((nmod, rows, 3 * D_MODEL), F32),
        grid=(nmod, 3 * D_MODEL // tn),
        in_specs=[
            pl.BlockSpec((rows, D_MODEL), lambda m, n: (0, 0)),
            pl.BlockSpec((1, D_MODEL, tn), lambda m, n: (m, 0, n)),
            pl.BlockSpec((1, 1, tn), lambda m, n: (m, 0, n)),
        ],
        out_specs=pl.BlockSpec((1, rows, tn), lambda m, n: (m, 0, n)),
        compiler_params=_cparams(("arbitrary", "arbitrary")),
        name="adaln_mod",
    )(c8, w, b)
    return out[:, :B, :]


def _rope_pieces(seq, dim):
    half = dim // 2
    inv = ROPE_THETA ** (-jnp.arange(0, dim, 2, dtype=F32) / dim)
    ang = jnp.arange(seq, dtype=F32)[:, None] * inv[None, :]
    cos, sin = jnp.cos(ang), jnp.sin(ang)
    pad = 64 - half
    one = [jnp.ones((seq, pad), F32)] if pad else []
    zero = [jnp.zeros((seq, pad), F32)] if pad else []
    return [cos] + one + [cos] + one + [-sin] + zero + [sin] + zero


def _place_cols(w, src):
    src = np.asarray(src)
    cols = jnp.take(w, jnp.asarray(np.maximum(src, 0)), axis=-1)
    return jnp.where(jnp.asarray(src >= 0), cols, 0.0)


def _att_layouts():
    off_kr = MLA_Q_RANK + MLA_KV_RANK
    off_sq = off_kr + MLA_ROPE
    off_sk = off_sq + SWA_HEADS * SWA_HD
    off_sv = off_sk + SWA_KV_HEADS * SWA_HD
    src = list(range(off_kr))
    grp = [-1] * LANES
    for i in range(16):
        grp[i] = off_kr + i
        grp[64 + i] = off_kr + 16 + i
    src += grp
    for base, nh in ((off_sq, SWA_HEADS), (off_sk, SWA_KV_HEADS)):
        for h in range(nh):
            grp = [-1] * LANES
            for i in range(32):
                grp[i] = base + h * SWA_HD + i
                grp[64 + i] = base + h * SWA_HD + 32 + i
            src += grp
    src += list(range(off_sv, off_sv + SWA_KV_HEADS * SWA_HD))
    uq, uk, uv = [], [], []
    for h in range(MLA_HEADS):
        qb = h * (MLA_NOPE + MLA_ROPE)
        kb = h * (MLA_NOPE + MLA_V)
        gq = [-1] * LANES
        gk = [-1] * LANES
        for i in range(16):
            gq[i] = qb + MLA_NOPE + i
            gq[64 + i] = qb + MLA_NOPE + 16 + i
        for i in range(48):
            gq[16 + i] = qb + i
            gk[16 + i] = kb + i
        for i in range(16):
            gq[80 + i] = qb + 48 + i
            gk[80 + i] = kb + 48 + i
        uq += gq
        uk += gk
        uv += list(range(kb + MLA_NOPE, kb + MLA_NOPE + MLA_V))
    return src, uq, uk, uv


def _att_in_kernel(x_ref, mod_ref, win_ref, qn_ref, wuq_ref, kvn_ref, wuk_ref, wuv_ref, tab_ref,
                   qm_ref, km_ref, vm_ref, qs_ref, ks_ref, vs_ref):
    mod = mod_ref[0]
    h = (x_ref[0] * (1.0 + mod[1:2, :]) + mod[0:1, :]).astype(BF16)
    z = _dot(h, win_ref[...])
    tab = tab_ref[...]
    t = lambda k: tab[:, k * LANES:(k + 1) * LANES]

    c_q = z[:, :MLA_Q_RANK]
    c_q = c_q * lax.rsqrt(jnp.mean(c_q * c_q, axis=-1, keepdims=True) + RMS_EPS) * qn_ref[...]
    q = _dot(c_q.astype(BF16), wuq_ref[...])
    c_kv = z[:, MLA_Q_RANK:MLA_Q_RANK + MLA_KV_RANK]
    c_kv = c_kv * lax.rsqrt(jnp.mean(c_kv * c_kv, axis=-1, keepdims=True) + RMS_EPS) * kvn_ref[...]
    c_kv = c_kv.astype(BF16)
    kup = _dot(c_kv, wuk_ref[...])
    v = _dot(c_kv, wuv_ref[...])
    off = MLA_Q_RANK + MLA_KV_RANK
    kr = _rope(z[:, off:off + LANES], t(0), t(1))
    for hd in range(MLA_HEADS):
        sl = slice(hd * LANES, (hd + 1) * LANES)
        qm_ref[0, hd] = (_rope(q[:, sl], t(0), t(1)) * MLA_Q_SCALE).astype(BF16)
        km_ref[0, hd] = (kup[:, sl] + kr).astype(BF16)
    for hp in range(MLA_HEADS // 2):
        vm_ref[0, hp, 0] = v[:, hp * LANES:(hp + 1) * LANES].T.astype(BF16)
    off += LANES
    for hd in range(SWA_HEADS):
        qs_ref[0, hd] = (_rope(z[:, off + hd * LANES:off + (hd + 1) * LANES], t(2), t(3))
                         * SWA_HD ** -0.5).astype(BF16)
    off += SWA_HEADS * LANES
    for kv in range(SWA_KV_HEADS):
        ks_ref[0, kv] = _rope(z[:, off + kv * LANES:off + (kv + 1) * LANES], t(2), t(3)).astype(BF16)
    off += SWA_KV_HEADS * LANES
    sv = z[:, off:off + LANES]
    vs_ref[0, 0] = sv.astype(BF16)
    vs_ref[0, 1] = pltpu.roll(sv, 64, 1).astype(BF16)


def _att_in(x, mod, lyr, w_in_p, q_norm, w_uq_p, kv_norm, w_uk_p, w_uv_p, tab, tm):
    B, S, D = x.shape
    bf = lambda *s: jax.ShapeDtypeStruct(s, BF16)
    tok4 = lambda nh: pl.BlockSpec((1, nh, tm, LANES), lambda b, i: (b, 0, i, 0))
    return pl.pallas_call(
        _att_in_kernel,
        out_shape=(bf(B, MLA_HEADS, S, LANES), bf(B, MLA_HEADS, S, LANES),
                   bf(B, MLA_HEADS // 2, S // tm, LANES, tm),
                   bf(B, SWA_HEADS, S, LANES), bf(B, SWA_KV_HEADS, S, LANES), bf(B, 2, S, LANES)),
        grid=(B, S // tm),
        in_specs=[
            pl.BlockSpec((1, tm, D), lambda b, i: (b, i, 0)),
            pl.BlockSpec((1, 8, D), lambda b, i: (b, 0, 0)),
            _layer_spec(w_in_p, lyr), _layer_spec(q_norm, lyr), _layer_spec(w_uq_p, lyr),
            _layer_spec(kv_norm, lyr), _layer_spec(w_uk_p, lyr), _layer_spec(w_uv_p, lyr),
            pl.BlockSpec((tm, tab.shape[1]), lambda b, i: (i, 0)),
        ],
        out_specs=(tok4(MLA_HEADS), tok4(MLA_HEADS),
                   pl.BlockSpec((1, MLA_HEADS // 2, 1, LANES, tm), lambda b, i: (b, 0, i, 0, 0)),
                   tok4(SWA_HEADS), tok4(SWA_KV_HEADS), tok4(2)),
        compiler_params=_cparams(("arbitrary", "arbitrary")),
        name="att_in_proj",
    )(x, mod, w_in_p, q_norm, w_uq_p, kv_norm, w_uk_p, w_uv_p, tab)


def _mla_kernel(q_ref, k_ref, vt_ref, o_ref, st0a, st0b, st1a, st1b, p0a, p0b, p1a, p1b, acc_sc, *, t, h):
    i = pl.program_id(2)
    assert t == 4 * h
    st_sets = ((st0a, st0b), (st1a, st1b))
    p_sets = ((p0a, p0b), (p1a, p1b))

    def scores(blk, st_wr):
        k0 = pl.multiple_of(blk * h, h)
        for hh in range(2):
            st_wr[hh] = _dot_nt(k_ref[0, hh, pl.ds(k0, h), :], q_ref[0, hh])

    def values(vt_blk, half, p_rd, hh):
        vt = vt_ref[0, 0, vt_blk, hh * MLA_V:(hh + 1) * MLA_V, half * h:(half + 1) * h]
        return _dot(vt, p_rd[hh])

    def accumulate(vt_blk, carry, p_rd):
        for hh in range(2):
            _, _, alpha_a, alpha_b = carry[hh]
            acc = alpha_a * acc_sc[hh] + values(vt_blk, 0, p_rd[0], hh)
            acc_sc[hh] = alpha_b * acc + values(vt_blk, 1, p_rd[1], hh)

    def softmax(carry, st_rd, p_wr, diag):
        out = []
        for hh in range(2):
            m, l, _, _ = carry[hh]
            alphas = []
            for half in range(2):
                st = st_rd[half][hh]
                if diag is not None:
                    key = lax.broadcasted_iota(jnp.int32, (h, t), 0) + (2 * diag + half) * h
                    qry = lax.broadcasted_iota(jnp.int32, (h, t), 1)
                    st = jnp.where(key <= qry, st, NEG_BIG)
                m_new = jnp.maximum(m, jnp.max(st, axis=0, keepdims=True))
                alphas.append(jnp.exp2(m - m_new))
                p = jnp.exp2(st - m_new)
                l = alphas[-1] * l + jnp.sum(p, axis=0, keepdims=True)
                p_wr[half][hh] = p.astype(BF16)
                m = m_new
            out.append((m, l, alphas[0], alphas[1]))
        return tuple(out)

    def iteration(c, par, carry, diag=None, more=True):
        if more:
            scores(2 * c + 2, st_sets[1 - par][0])
            scores(2 * c + 3, st_sets[1 - par][1])
        accumulate(jnp.maximum(c - 1, 0), carry, p_sets[1 - par])
        return softmax(carry, st_sets[par], p_sets[par], diag)

    def two_pairs(cc, carry):
        carry = iteration(2 * cc, 0, carry)
        return iteration(2 * cc + 1, 1, carry)

    scores(0, st0a)
    scores(1, st0b)
    p1a[...] = jnp.zeros(p1a.shape, BF16)
    p1b[...] = jnp.zeros(p1b.shape, BF16)
    acc_sc[...] = jnp.zeros(acc_sc.shape, F32)
    one = jnp.ones((1, t), F32)
    init = (jnp.full((1, t), NEG_BIG, F32), jnp.zeros((1, t), F32), one, one)
    carry = lax.fori_loop(0, i, two_pairs, (init, init))
    carry = iteration(2 * i, 0, carry, diag=0)
    carry = iteration(2 * i + 1, 1, carry, diag=1, more=False)
    accumulate(2 * i + 1, carry, p_sets[1])
    outs = [acc_sc[hh] / carry[hh][1] for hh in range(2)]
    o_ref[0] = jnp.concatenate(outs, axis=0).T.astype(BF16)


def _mla_attention(qm, km, vt, t):
    B, H, S, _ = qm.shape
    tv = vt.shape[-1]
    h = tv // 2
    assert t % tv == 0
    return pl.pallas_call(
        functools.partial(_mla_kernel, t=t, h=h),
        out_shape=jax.ShapeDtypeStruct((B, S, H * MLA_V), BF16),
        grid=(B, H // 2, S // t),
        in_specs=[
            pl.BlockSpec((1, 2, t, LANES), lambda b, hp, i: (b, hp, i, 0)),
            pl.BlockSpec((1, 2, S, LANES), lambda b, hp, i: (b, hp, 0, 0)),
            pl.BlockSpec((1, 1, S // tv, LANES, tv), lambda b, hp, i: (b, hp, 0, 0, 0)),
        ],
        out_specs=pl.BlockSpec((1, t, LANES), lambda b, hp, i: (b, i, hp)),
        scratch_shapes=([pltpu.VMEM((2, h, t), F32)] * 4 + [pltpu.VMEM((2, h, t), BF16)] * 4
                        + [pltpu.VMEM((2, MLA_V, t), F32)]),
        compiler_params=_cparams(("arbitrary", "arbitrary", "arbitrary")),
        name="mla_flash",
    )(qm, km, vt)


def _swa_kernel(sink_ref, q_ref, kc_ref, kp_ref, vc_ref, vp_ref, o_ref, *, tq):
    W = SWA_WINDOW
    G = SWA_HEADS // SWA_KV_HEADS
    i = pl.program_id(1)
    nsub = tq // W
    r = lax.broadcasted_iota(jnp.int32, (G * W, 2 * W), 0) % W
    c = lax.broadcasted_iota(jnp.int32, (G * W, 2 * W), 1)
    band = (c > r) & (c <= r + W)
    first = band & ((i > 0) | (c >= W))
    lane = lax.broadcasted_iota(jnp.int32, (W, LANES), 1)
    for kv in range(SWA_KV_HEADS):
        kcat = jnp.concatenate([kp_ref[0, kv], kc_ref[0, kv]], axis=0)
        va = jnp.concatenate([vp_ref[0, 0], vc_ref[0, 0]], axis=0)
        vb = jnp.concatenate([vp_ref[0, 1], vc_ref[0, 1]], axis=0)
        v_even, v_odd = (va, vb) if kv == 0 else (vb, va)
        sink = jnp.concatenate(
            [jnp.full((W, 1), sink_ref[kv * G + g], F32) for g in range(G)], axis=0)
        for n in range(nsub):
            q4 = jnp.concatenate([q_ref[0, kv * G + g, n * W:(n + 1) * W, :] for g in range(G)], axis=0)
            s = _dot_nt(q4, kcat[n * W:(n + 2) * W])
            s = jnp.where(first if n == 0 else band, s, NEG_BIG)
            m = jnp.maximum(jnp.max(s, axis=-1, keepdims=True), sink)
            p = jnp.exp(s - m)
            den = jnp.sum(p, axis=-1, keepdims=True) + jnp.exp(sink - m)
            p = (p / den).astype(BF16)
            for pr in range(G // 2):
                oe = _dot(p[(2 * pr) * W:(2 * pr + 1) * W], v_even[n * W:(n + 2) * W])
                oo = _dot(p[(2 * pr + 1) * W:(2 * pr + 2) * W], v_odd[n * W:(n + 2) * W])
                blk = kv * (G // 2) + pr
                o_ref[0, n * W:(n + 1) * W, blk * LANES:(blk + 1) * LANES] = (
                    jnp.where(lane < SWA_HD, oe, oo).astype(BF16))


def _swa_attention(sinks, qs, ks, vs, tq):
    B, _, S, _ = qs.shape
    W = SWA_WINDOW
    nb = tq // W
    cur = lambda nh: pl.BlockSpec((1, nh, tq, LANES), lambda b, i: (b, 0, i, 0))
    prev = lambda nh: pl.BlockSpec((1, nh, W, LANES), lambda b, i: (b, 0, jnp.maximum(i * nb - 1, 0), 0))
    return pl.pallas_call(
        functools.partial(_swa_kernel, tq=tq),
        out_shape=jax.ShapeDtypeStruct((B, S, SWA_HEADS * SWA_HD), BF16),
        grid=(B, S // tq),
        in_specs=[
            pl.BlockSpec(memory_space=pltpu.SMEM),
            cur(SWA_HEADS), cur(SWA_KV_HEADS), prev(SWA_KV_HEADS), cur(2), prev(2),
        ],
        out_specs=pl.BlockSpec((1, tq, SWA_HEADS * SWA_HD), lambda b, i: (b, i, 0)),
        compiler_params=_cparams(("arbitrary", "arbitrary")),
        name="swa_attn",
    )(sinks, qs, ks, ks, vs, vs)


def _mlp_tail(x, y, mod, w1_ref, w2_ref):
    x1 = _layer_norm(DN_ALPHA * x + (1.0 + mod[0:1, :]) * y, mod[4:5, :], mod[5:6, :])
    h = (x1 * (1.0 + mod[2:3, :]) + mod[1:2, :]).astype(BF16)
    acc = None
    for cidx in range(D_FF // FF_CHUNK):
        hid = _dot(h, w1_ref[:, cidx * FF_CHUNK:(cidx + 1) * FF_CHUNK])
        hid = jnp.square(jnp.maximum(hid, 0.0)).astype(BF16)
        part = _dot(hid, w2_ref[cidx * FF_CHUNK:(cidx + 1) * FF_CHUNK, :])
        acc = part if acc is None else acc + part
    return _layer_norm(DN_ALPHA * x1 + (1.0 + mod[3:4, :]) * acc, mod[6:7, :], mod[7:8, :])


def _att_post_kernel(x_ref, oa_ref, ob_ref, mod_ref, wout_ref, w1_ref, w2_ref, o_ref):
    half = wout_ref.shape[0] // 2
    y = _dot(oa_ref[0], wout_ref[:half, :]) + _dot(ob_ref[0], wout_ref[half:, :])
    o_ref[0] = _mlp_tail(x_ref[0], y, mod_ref[0], w1_ref, w2_ref)


def _rec_post_kernel(x_ref, oc_ref, ys_ref, u_ref, mod_ref, d_ref, gw_ref, gb_ref,
                     wout_ref, w1_ref, w2_ref, o_ref):
    half = wout_ref.shape[0] // 2
    yy = jax.nn.gelu(ys_ref[0] + d_ref[...] * u_ref[0])
    od = yy * jax.nn.sigmoid(_dot(yy.astype(BF16), gw_ref[...]) + gb_ref[...])
    y = _dot(oc_ref[0], wout_ref[:half, :]) + _dot(od.astype(BF16), wout_ref[half:, :])
    o_ref[0] = _mlp_tail(x_ref[0], y, mod_ref[0], w1_ref, w2_ref)


def _post(kernel, x, toks, mod, params, tm, name):
    B, S, D = x.shape
    tok = lambda a: pl.BlockSpec((1, tm, a.shape[-1]), lambda b, i: (b, i, 0))
    return pl.pallas_call(
        kernel,
        out_shape=jax.ShapeDtypeStruct((B, S, D), F32),
        grid=(B, S // tm),
        in_specs=([tok(x)] + [tok(a) for a in toks]
                  + [pl.BlockSpec((1, 8, D), lambda b, i: (b, 0, 0))]
                  + [_layer_spec(a, lyr) for a, lyr in params]),
        out_specs=tok(x),
        compiler_params=_cparams(("arbitrary", "arbitrary")),
        name=name,
    )(x, *toks, mod, *[a for a, _ in params])


def _rec_in_kernel(x_ref, mod_ref, win_ref, tab_ref, q_ref, k_ref, v_ref, g_ref, u_ref):
    mod = mod_ref[0]
    h = (x_ref[0] * (1.0 + mod[1:2, :]) + mod[0:1, :]).astype(BF16)
    z = _dot(h, win_ref[...])
    tab = tab_ref[...]
    t = lambda k: tab[:, k * LANES:(k + 1) * LANES]
    W = RET_HEADS * RET_DK
    for hd in range(RET_HEADS):
        sl = slice(hd * LANES, (hd + 1) * LANES)
        q_ref[0, :, sl] = _rope(z[:, hd * LANES:(hd + 1) * LANES], t(0), t(1)).astype(BF16)
        k_ref[0, :, sl] = _rope(z[:, W + hd * LANES:W + (hd + 1) * LANES], t(0), t(1)) * RET_DK ** -0.5
    v_ref[0] = z[:, 2 * W:3 * W].astype(BF16)
    g_ref[0] = z[:, 3 * W:4 * W]
    u_ref[0] = z[:, 4 * W:]


def _rec_in(x, mod, lyr, w_in, tab, tm):
    B, S, D = x.shape
    W = RET_HEADS * RET_DK
    sd = lambda dt: jax.ShapeDtypeStruct((B, S, W), dt)
    tok = pl.BlockSpec((1, tm, W), lambda b, i: (b, i, 0))
    return pl.pallas_call(
        _rec_in_kernel,
        out_shape=(sd(BF16), sd(F32), sd(BF16), sd(F32), sd(F32)),
        grid=(B, S // tm),
        in_specs=[
            pl.BlockSpec((1, tm, D), lambda b, i: (b, i, 0)),
            pl.BlockSpec((1, 8, D), lambda b, i: (b, 0, 0)),
            _layer_spec(w_in, lyr),
            pl.BlockSpec((tm, tab.shape[1]), lambda b, i: (i, 0)),
        ],
        out_specs=(tok, tok, tok, tok, tok),
        compiler_params=_cparams(("arbitrary", "arbitrary")),
        name="rec_in_proj",
    )(x, mod, w_in, tab)


def _ret_kernel(q_ref, k_ref, v_ref, g_ref, dec_ref, te_ref, fs_ref, cd_ref, o_ref, st_sc, *, tm):
    C = RET_CHUNK

    @pl.when(pl.program_id(1) == 0)
    def _():
        st_sc[...] = jnp.zeros(st_sc.shape, F32)

    for hd in range(RET_HEADS):
        sl = slice(hd * LANES, (hd + 1) * LANES)
        state = st_sc[hd]
        for n in range(tm // C):
            rows = slice(n * C, (n + 1) * C)
            q = q_ref[0, rows, sl]
            k = k_ref[0, rows, sl]
            v = v_ref[0, rows, sl]
            sc = _dot_nt(q, k.astype(BF16)) * dec_ref[hd]
            o = _dot(sc.astype(BF16), v) + _dot(q, state.astype(BF16)) * fs_ref[hd]
            upd = _dot_tn((k * te_ref[hd]).astype(BF16), v)
            state = cd_ref[hd] * state + upd
            mu = jnp.mean(o, axis=-1, keepdims=True)
            d = o - mu
            var = jnp.mean(d * d, axis=-1, keepdims=True)
            o = d * lax.rsqrt(var + LN_EPS)
            o_ref[0, rows, sl] = (jax.nn.silu(g_ref[0, rows, sl]) * o).astype(BF16)
        st_sc[hd] = state


def _retention(q, k, v, g, tm):
    B, S, W = q.shape
    H, C = RET_HEADS, RET_CHUNK
    log_gamma = jnp.log(1.0 - 2.0 ** (-5.0 - jnp.arange(H, dtype=F32)))
    idx = jnp.arange(C, dtype=F32)
    diff = idx[:, None] - idx[None, :]
    decay = jnp.where(diff >= 0, jnp.exp(log_gamma[:, None, None] * jnp.maximum(diff, 0.0)), 0.0)
    to_end = jnp.exp(log_gamma[:, None] * (C - 1.0 - idx)[None, :])
    from_start = jnp.exp((idx + 1.0)[None, :] * log_gamma[:, None])
    chunk_decay = jnp.exp(log_gamma * C)
    bc = lambda a: jnp.broadcast_to(a[:, :, None], (H, C, LANES))
    cd = jnp.broadcast_to(chunk_decay[:, None, None], (H, RET_DK, RET_DV))
    tok = pl.BlockSpec((1, tm, W), lambda b, i: (b, i, 0))
    tbl = _const_spec((H, C, LANES))
    return pl.pallas_call(
        functools.partial(_ret_kernel, tm=tm),
        out_shape=jax.ShapeDtypeStruct((B, S, W), BF16),
        grid=(B, S // tm),
        in_specs=[tok, tok, tok, tok, tbl, tbl, tbl, tbl],
        out_specs=tok,
        scratch_shapes=[pltpu.VMEM((H, RET_DK, RET_DV), F32)],
        compiler_params=_cparams(("arbitrary", "arbitrary")),
        name="retention",
    )(q, k, v, g, decay, bc(to_end), bc(from_start), cd)


def _s5_kernel(u_ref, kern_ref, we_ref, wc_ref, mc_ref, ms_ref, y_ref, toep_sc, *, nchunk, nsteps):
    L = S5_CHUNK
    s_i = lax.broadcasted_iota(jnp.int32, (L, L), 0)
    t_i = lax.broadcasted_iota(jnp.int32, (L, L), 1)
    causal = t_i >= s_i
    for q in range(S5_GROUP):
        taps = kern_ref[0, q]
        for p in range(S5_GROUP):
            blk = pltpu.roll(jnp.broadcast_to(taps[p:p + 1, :], (L, L)), 0, 1, stride=1, stride_axis=0)
            toep_sc[q * L:(q + 1) * L, p * L:(p + 1) * L] = jnp.where(causal, blk, 0.0).astype(BF16)
    u = u_ref[0]
    R = u.shape[0]
    y = _dot(u, toep_sc[...])
    x = _dot(u, we_ref[0])
    cidx = lax.broadcasted_iota(jnp.int32, (R, LANES), 0) % nchunk
    mc = mc_ref[0]
    ms = ms_ref[0]
    for kk in range(nsteps):
        sh = 1 << kk
        xs = jnp.where(cidx >= sh, pltpu.roll(x, sh, 0), 0.0)
        x = x + xs * mc[kk:kk + 1, :] + pltpu.roll(xs, 64, 1) * ms[kk:kk + 1, :]
    xprev = jnp.where(cidx >= 1, pltpu.roll(x, 1, 0), 0.0)
    y_ref[0] = y + _dot(xprev.astype(BF16), wc_ref[0])


def _s5_prep(a_re, a_im, log_step, b_re, b_im, c_re, c_im, L, nsteps):
    G, N, P = S5_GROUPS, S5_STATE, S5_GROUP
    dt = jnp.exp(log_step)[:, None]
    lr, li = a_re, a_im
    la, th = lr * dt, li * dt
    mag = jnp.exp(la)
    ar, ai = mag * jnp.cos(th), mag * jnp.sin(th)
    den = lr * lr + li * li
    cr = ((ar - 1.0) * lr + ai * li) / den
    ci = (ai * lr - (ar - 1.0) * li) / den
    bbr = cr[..., None] * b_re - ci[..., None] * b_im
    bbi = cr[..., None] * b_im + ci[..., None] * b_re
    tau = jnp.arange(L + 1, dtype=F32)[:, None, None]
    pm = jnp.exp(la[None] * tau)
    pr, pi = pm * jnp.cos(th[None] * tau), pm * jnp.sin(th[None] * tau)
    car = c_re[None] * pr[:, :, None, :] - c_im[None] * pi[:, :, None, :]
    cai = c_re[None] * pi[:, :, None, :] + c_im[None] * pr[:, :, None, :]
    kern = (jnp.einsum('tgpn,gnq->gqpt', car[:L], bbr, precision=HI)
            - jnp.einsum('tgpn,gnq->gqpt', cai[:L], bbi, precision=HI))
    rev_r, rev_i = pr[L - 1::-1][:L], pi[L - 1::-1][:L]
    we_r = rev_r[..., None] * bbr[None] - rev_i[..., None] * bbi[None]
    we_i = rev_r[..., None] * bbi[None] + rev_i[..., None] * bbr[None]
    we = jnp.concatenate([we_r, we_i], axis=2).transpose(1, 3, 0, 2).reshape(G, P * L, 2 * N)
    wc = jnp.concatenate([car[1:], -cai[1:]], axis=3)
    wc = wc.transpose(1, 3, 2, 0).reshape(G, 2 * N, P * L)
    zr, zi = pr[L], pi[L]
    mcs, mss = [], []
    for _ in range(nsteps):
        mcs.append(jnp.concatenate([zr, zr], axis=-1))
        mss.append(jnp.concatenate([-zi, zi], axis=-1))
        zr, zi = zr * zr - zi * zi, 2.0 * zr * zi
    mc = jnp.stack(mcs, axis=1)
    ms = jnp.stack(mss, axis=1)
    return kern, we.astype(BF16), wc.astype(BF16), mc, ms


def _s5_scan(u, a_re, a_im, log_step, b_re, b_im, c_re, c_im):
    B, S, W = u.shape
    G, P, N, L = S5_GROUPS, S5_GROUP, S5_STATE, S5_CHUNK
    nchunk = S // L
    nsteps = max(1, (nchunk - 1).bit_length())
    kern, we, wc, mc, ms = _s5_prep(a_re, a_im, log_step, b_re, b_im, c_re, c_im, L, nsteps)
    R = B * nchunk
    ug = u.astype(BF16).reshape(B, nchunk, L, G, P).transpose(3, 0, 1, 4, 2).reshape(G, R, P * L)
    grp = lambda a: pl.BlockSpec((1,) + a.shape[1:], lambda g: (g,) + (0,) * (a.ndim - 1))
    y = pl.pallas_call(
        functools.partial(_s5_kernel, nchunk=nchunk, nsteps=nsteps),
        out_shape=jax.ShapeDtypeStruct((G, R, P * L), F32),
        grid=(G,),
        in_specs=[grp(ug), grp(kern), grp(we), grp(wc), grp(mc), grp(ms)],
        out_specs=pl.BlockSpec((1, R, P * L), lambda g: (g, 0, 0)),
        scratch_shapes=[pltpu.VMEM((P * L, P * L), BF16)],
        compiler_params=_cparams(("arbitrary",)),
        name="s5_conv",
    )(ug, kern, we, wc, mc, ms)
    return y.reshape(G, B, nchunk, P, L).transpose(1, 2, 4, 0, 3).reshape(B, S, W)


def _pick_tile(S, pref):
    t = min(pref, S)
    assert S % t == 0, (S, t)
    return t


def kernel(x, c, ada_w, ada_b, ln_g, ln_b, att_w_in, mla_q_norm, mla_w_uq, mla_kv_norm, mla_w_ukv, swa_sinks, att_w_out, rec_w_in, s5_a_re, s5_a_im, s5_log_step, s5_b_re, s5_b_im, s5_c_re, s5_c_im, s5_d, s5_glu_w, s5_glu_b, rec_w_out, mlp_w1, mlp_w2):
    B, S, D = x.shape
    assert D == D_MODEL and S % SWA_WINDOW == 0 and S % S5_CHUNK == 0
    tm = _pick_tile(S, 512)

    mods = _modulation(c, ada_w, ada_b)
    zeros = jnp.zeros((B, D), F32)

    def in_mod(m):
        rows = [m[:, :D], m[:, D:2 * D]] + [zeros] * 6
        return jnp.stack(rows, axis=1)

    def post_mod(l, m1, m2):
        bc = lambda v: jnp.broadcast_to(v[None, :], (B, D))
        rows = [m1[:, 2 * D:], m2[:, :D], m2[:, D:2 * D], m2[:, 2 * D:],
                bc(ln_g[l, 0]), bc(ln_b[l, 0]), bc(ln_g[l, 1]), bc(ln_b[l, 1])]
        return jnp.stack(rows, axis=1)

    src_in, src_uq, src_uk, src_uv = _att_layouts()
    att_tab = jnp.concatenate(_rope_pieces(S, MLA_ROPE) + _rope_pieces(S, SWA_HD), axis=1)
    rec_tab = jnp.concatenate(_rope_pieces(S, RET_DK), axis=1)

    w_in_att = _place_cols(att_w_in, src_in).astype(BF16)
    w_uq = _place_cols(mla_w_uq, src_uq).astype(BF16)
    w_uk = _place_cols(mla_w_ukv, src_uk).astype(BF16)
    w_uv = _place_cols(mla_w_ukv, src_uv).astype(BF16)
    q_norm = mla_q_norm[:, None, :]
    kv_norm = mla_kv_norm[:, None, :]
    w_out_att = att_w_out.astype(BF16)
    w_in_rec = rec_w_in.astype(BF16)
    w_out_rec = rec_w_out.astype(BF16)
    glu_w = s5_glu_w.astype(BF16)
    s5_dd = s5_d[:, None, :]
    glu_b = s5_glu_b[:, None, :]
    w1 = mlp_w1.astype(BF16)
    w2 = mlp_w2.astype(BF16)

    for l in range(DEPTH):
        j = l // 2
        m1, m2 = mods[2 * l], mods[2 * l + 1]
        pmod = post_mod(l, m1, m2)
        if l % 2 == 0:
            qm, km, vt, qs, ks, vs = _att_in(x, in_mod(m1), j, w_in_att, q_norm, w_uq, kv_norm, w_uk, w_uv,
                                             att_tab, tm)
            o_a = _mla_attention(qm, km, vt, _pick_tile(S, 1024))
            o_b = _swa_attention(swa_sinks[j], qs, ks, vs, tm)
            x = _post(_att_post_kernel, x, [o_a, o_b], pmod,
                      [(w_out_att, j), (w1, l), (w2, l)], tm, "att_post")
        else:
            rq, rk, rv, rg, u = _rec_in(x, in_mod(m1), j, w_in_rec, rec_tab, tm)
            o_c = _retention(rq, rk, rv, rg, tm)
            ys = _s5_scan(u, s5_a_re[j], s5_a_im[j], s5_log_step[j], s5_b_re[j], s5_b_im[j],
                          s5_c_re[j], s5_c_im[j])
            x = _post(_rec_post_kernel, x, [o_c, ys, u], pmod,
                      [(s5_dd, j), (glu_w, j), (glu_b, j), (w_out_rec, j), (w1, l), (w2, l)], tm, "rec_post")
    return x
```

```python
import functools
import math

import jax
import jax.numpy as jnp
import numpy as np
from jax import lax
from jax.experimental import pallas as pl
from jax.experimental.pallas import tpu as pltpu

F32 = jnp.float32
BF16 = jnp.bfloat16

D_MODEL = 1024
DEPTH = 4
DN_ALPHA = (2.0 * DEPTH) ** 0.25
LN_EPS = 1e-5
RMS_EPS = 1e-6
ROPE_THETA = 10000.0
MLA_HEADS = 8
MLA_NOPE = 64
MLA_ROPE = 32
MLA_V = 64
MLA_Q_RANK = 384
MLA_KV_RANK = 256
SWA_HEADS = 8
SWA_KV_HEADS = 2
SWA_HD = 64
SWA_WINDOW = 128
RET_HEADS = 4
RET_DK = 128
RET_DV = 128
RET_CHUNK = 128
S5_WIDTH = 512
S5_GROUP = 16
S5_GROUPS = 32
S5_STATE = 64
S5_CHUNK = 128
D_FF = 4096
FF_CHUNK = 1024

LANES = 128
VMEM_LIMIT = 56 * 1024 * 1024
NEG_BIG = -1e30
HI = lax.Precision.HIGHEST
MLA_VX = MLA_V + 16
MLA_Q_SCALE = (MLA_NOPE + MLA_ROPE) ** -0.5 * math.log2(math.e)


def _cparams(sem):
    return pltpu.CompilerParams(dimension_semantics=sem, vmem_limit_bytes=VMEM_LIMIT)


def _const_spec(shape):
    nd = len(shape)
    return pl.BlockSpec(shape, lambda *_: (0,) * nd, pipeline_mode=pl.Buffered(1))


def _layer_spec(arr, lyr):
    return pl.BlockSpec((None,) + arr.shape[1:], lambda *_: (lyr, 0, 0), pipeline_mode=pl.Buffered(1))


def _dot(a, b):
    return jnp.dot(a, b, preferred_element_type=F32)


def _dot_nt(a, b):
    return lax.dot_general(a, b, (((1,), (1,)), ((), ())), preferred_element_type=F32)


def _dot_tn(a, b):
    return lax.dot_general(a, b, (((0,), (0,)), ((), ())), preferred_element_type=F32)


def _rope(x, c, s):
    return x * c + pltpu.roll(x, 64, 1) * s


def _layer_norm(v, g, b):
    mu = jnp.mean(v, axis=-1, keepdims=True)
    d = v - mu
    var = jnp.mean(d * d, axis=-1, keepdims=True)
    return d * lax.rsqrt(var + LN_EPS) * g + b


def _mod_kernel(c_ref, w_ref, b_ref, o_ref):
    cond = jax.nn.silu(c_ref[...]).astype(BF16)
    o_ref[0] = _dot(cond, w_ref[0].astype(BF16)) + b_ref[0]


def _modulation(c, ada_w, ada_b):
    B = c.shape[0]
    nmod = ada_w.shape[0] * ada_w.shape[1]
    rows = 8
    tn = 768
    c8 = jnp.zeros((rows, D_MODEL), F32).at[:B].set(c)
    w = ada_w.reshape(nmod, D_MODEL, 3 * D_MODEL)
    b = ada_b.reshape(nmod, 1, 3 * D_MODEL)
    out = pl.pallas_call(
        _mod_kernel,
        out_shape=jax.ShapeDtypeStruct((nmod, rows, 3 * D_MODEL), F32),
        grid=(nmod, 3 * D_MODEL // tn),
        in_specs=[
            pl.BlockSpec((rows, D_MODEL), lambda m, n: (0, 0)),
            pl.BlockSpec((1, D_MODEL, tn), lambda m, n: (m, 0, n)),
            pl.BlockSpec((1, 1, tn), lambda m, n: (m, 0, n)),
        ],
        out_specs=pl.BlockSpec((1, rows, tn), lambda m, n: (m, 0, n)),
        compiler_params=_cparams(("arbitrary", "arbitrary")),
        name="adaln_mod",
    )(c8, w, b)
    return out[:, :B, :]


def _rope_pieces(seq, dim):
    half = dim // 2
    inv = ROPE_THETA ** (-jnp.arange(0, dim, 2, dtype=F32) / dim)
    ang = jnp.arange(seq, dtype=F32)[:, None] * inv[None, :]
    cos, sin = jnp.cos(ang), jnp.sin(ang)
    pad = 64 - half
    one = [jnp.ones((seq, pad), F32)] if pad else []
    zero = [jnp.zeros((seq, pad), F32)] if pad else []
    return [cos] + one + [cos] + one + [-sin] + zero + [sin] + zero


def _place_cols(w, src):
    src = np.asarray(src)
    cols = jnp.take(w, jnp.asarray(np.maximum(src, 0)), axis=-1)
    return jnp.where(jnp.asarray(src >= 0), cols, 0.0)


def _att_layouts():
    off_kr = MLA_Q_RANK + MLA_KV_RANK
    off_sq = off_kr + MLA_ROPE
    off_sk = off_sq + SWA_HEADS * SWA_HD
    off_sv = off_sk + SWA_KV_HEADS * SWA_HD
    src = list(range(off_kr))
    grp = [-1] * LANES
    for i in range(16):
        grp[i] = off_kr + i
        grp[64 + i] = off_kr + 16 + i
    src += grp
    for base, nh in ((off_sq, SWA_HEADS), (off_sk, SWA_KV_HEADS)):
        for h in range(nh):
            grp = [-1] * LANES
            for i in range(32):
                grp[i] = base + h * SWA_HD + i
                grp[64 + i] = base + h * SWA_HD + 32 + i
            src += grp
    src += list(range(off_sv, off_sv + SWA_KV_HEADS * SWA_HD))
    uq, uk, uv = [], [], []
    for h in range(MLA_HEADS):
        qb = h * (MLA_NOPE + MLA_ROPE)
        kb = h * (MLA_NOPE + MLA_V)
        gq = [-1] * LANES
        gk = [-1] * LANES
        for i in range(16):
            gq[i] = qb + MLA_NOPE + i
            gq[64 + i] = qb + MLA_NOPE + 16 + i
        for i in range(48):
            gq[16 + i] = qb + i
            gk[16 + i] = kb + i
        for i in range(16):
            gq[80 + i] = qb + 48 + i
            gk[80 + i] = kb + 48 + i
        uq += gq
        uk += gk
        uv += list(range(kb + MLA_NOPE, kb + MLA_NOPE + MLA_V))
    return src, uq, uk, uv


def _att_in_kernel(x_ref, mod_ref, win_ref, qn_ref, wuq_ref, kvn_ref, wuk_ref, wuv_ref, tab_ref,
                   qm_ref, km_ref, vm_ref, qs_ref, ks_ref, vs_ref):
    mod = mod_ref[0]
    h = (x_ref[0] * (1.0 + mod[1:2, :]) + mod[0:1, :]).astype(BF16)
    z = _dot(h, win_ref[...])
    tab = tab_ref[...]
    t = lambda k: tab[:, k * LANES:(k + 1) * LANES]

    c_q = z[:, :MLA_Q_RANK]
    c_q = c_q * lax.rsqrt(jnp.mean(c_q * c_q, axis=-1, keepdims=True) + RMS_EPS) * qn_ref[...]
    q = _dot(c_q.astype(BF16), wuq_ref[...])
    c_kv = z[:, MLA_Q_RANK:MLA_Q_RANK + MLA_KV_RANK]
    c_kv = c_kv * lax.rsqrt(jnp.mean(c_kv * c_kv, axis=-1, keepdims=True) + RMS_EPS) * kvn_ref[...]
    c_kv = c_kv.astype(BF16)
    kup = _dot(c_kv, wuk_ref[...])
    v = _dot(c_kv, wuv_ref[...])
    off = MLA_Q_RANK + MLA_KV_RANK
    kr = _rope(z[:, off:off + LANES], t(0), t(1))
    for hd in range(MLA_HEADS):
        sl = slice(hd * LANES, (hd + 1) * LANES)
        qm_ref[0, hd] = (_rope(q[:, sl], t(0), t(1)) * MLA_Q_SCALE).astype(BF16)
        km_ref[0, hd] = (kup[:, sl] + kr).astype(BF16)
    tm = v.shape[0]
    ones_row = jnp.where(lax.broadcasted_iota(jnp.int32, (MLA_VX - MLA_V, tm), 0) == 0, 1.0, 0.0)
    for hp in range(MLA_HEADS // 2):
        vt = v[:, hp * LANES:(hp + 1) * LANES].T
        vm_ref[0, hp, 0] = jnp.concatenate(
            [vt[:MLA_V], ones_row, vt[MLA_V:], ones_row], axis=0).astype(BF16)
    off += LANES
    for hd in range(SWA_HEADS):
        qs_ref[0, hd] = (_rope(z[:, off + hd * LANES:off + (hd + 1) * LANES], t(2), t(3))
                         * SWA_HD ** -0.5).astype(BF16)
    off += SWA_HEADS * LANES
    for kv in range(SWA_KV_HEADS):
        ks_ref[0, kv] = _rope(z[:, off + kv * LANES:off + (kv + 1) * LANES], t(2), t(3)).astype(BF16)
    off += SWA_KV_HEADS * LANES
    sv = z[:, off:off + LANES]
    vs_ref[0, 0] = sv.astype(BF16)
    vs_ref[0, 1] = pltpu.roll(sv, 64, 1).astype(BF16)


def _att_in(x, mod, lyr, w_in_p, q_norm, w_uq_p, kv_norm, w_uk_p, w_uv_p, tab, tm):
    B, S, D = x.shape
    bf = lambda *s: jax.ShapeDtypeStruct(s, BF16)
    tok4 = lambda nh: pl.BlockSpec((1, nh, tm, LANES), lambda b, i: (b, 0, i, 0))
    return pl.pallas_call(
        _att_in_kernel,
        out_shape=(bf(B, MLA_HEADS, S, LANES), bf(B, MLA_HEADS, S, LANES),
                   bf(B, MLA_HEADS // 2, S // tm, 2 * MLA_VX, tm),
                   bf(B, SWA_HEADS, S, LANES), bf(B, SWA_KV_HEADS, S, LANES), bf(B, 2, S, LANES)),
        grid=(B, S // tm),
        in_specs=[
            pl.BlockSpec((1, tm, D), lambda b, i: (b, i, 0)),
            pl.BlockSpec((1, 8, D), lambda b, i: (b, 0, 0)),
            _layer_spec(w_in_p, lyr), _layer_spec(q_norm, lyr), _layer_spec(w_uq_p, lyr),
            _layer_spec(kv_norm, lyr), _layer_spec(w_uk_p, lyr), _layer_spec(w_uv_p, lyr),
            pl.BlockSpec((tm, tab.shape[1]), lambda b, i: (i, 0)),
        ],
        out_specs=(tok4(MLA_HEADS), tok4(MLA_HEADS),
                   pl.BlockSpec((1, MLA_HEADS // 2, 1, 2 * MLA_VX, tm), lambda b, i: (b, 0, i, 0, 0)),
                   tok4(SWA_HEADS), tok4(SWA_KV_HEADS), tok4(2)),
        compiler_params=_cparams(("arbitrary", "arbitrary")),
        name="att_in_proj",
    )(x, mod, w_in_p, q_norm, w_uq_p, kv_norm, w_uk_p, w_uv_p, tab)


def _mla_kernel(q_ref, k_ref, vt_ref, o_ref, st0a, st0b, st1a, st1b, p0a, p0b, p1a, p1b, acc_sc, cm_sc,
                *, t, h):
    i = pl.program_id(2)
    assert t == 4 * h
    st_sets = ((st0a, st0b), (st1a, st1b))
    p_sets = ((p0a, p0b), (p1a, p1b))

    def scores(blk, st_set, half):
        k0 = pl.multiple_of(blk * h, h)
        for hh in range(2):
            st = _dot_nt(k_ref[0, hh, pl.ds(k0, h), :], q_ref[0, hh])
            st_sets[st_set][half][hh] = st
            cm_sc[(st_set * 2 + half) * 2 + hh, 0:1, :] = jnp.max(st, axis=0, keepdims=True)

    def values(vt_blk, half, p_rd, hh):
        vt = vt_ref[0, 0, vt_blk, hh * MLA_VX:(hh + 1) * MLA_VX, half * h:(half + 1) * h]
        return _dot(vt, p_rd[hh])

    def accumulate(vt_blk, carry, p_rd):
        for hh in range(2):
            _, alpha_a, alpha_b = carry[hh]
            acc = alpha_a * acc_sc[hh] + values(vt_blk, 0, p_rd[0], hh)
            acc_sc[hh] = alpha_b * acc + values(vt_blk, 1, p_rd[1], hh)

    def softmax(carry, par, diag):
        out = []
        for hh in range(2):
            m = carry[hh][0]
            alphas = []
            for half in range(2):
                st = st_sets[par][half][hh]
                if diag is None:
                    blk_max = cm_sc[(par * 2 + half) * 2 + hh, 0:1, :]
                else:
                    key = lax.broadcasted_iota(jnp.int32, (h, t), 0) + (2 * diag + half) * h
                    qry = lax.broadcasted_iota(jnp.int32, (h, t), 1)
                    st = jnp.where(key <= qry, st, NEG_BIG)
                    blk_max = jnp.max(st, axis=0, keepdims=True)
                m_new = jnp.maximum(m, blk_max)
                alphas.append(jnp.exp2(m - m_new))
                p_sets[par][half][hh] = jnp.exp2(st - m_new).astype(BF16)
                m = m_new
            out.append((m, alphas[0], alphas[1]))
        return tuple(out)

    def iteration(c, par, carry, diag=None, more=True):
        if more:
            scores(2 * c + 2, 1 - par, 0)
            scores(2 * c + 3, 1 - par, 1)
        accumulate(jnp.maximum(c - 1, 0), carry, p_sets[1 - par])
        return softmax(carry, par, diag)

    def two_pairs(cc, carry):
        carry = iteration(2 * cc, 0, carry)
        return iteration(2 * cc + 1, 1, carry)

    scores(0, 0, 0)
    scores(1, 0, 1)
    p1a[...] = jnp.zeros(p1a.shape, BF16)
    p1b[...] = jnp.zeros(p1b.shape, BF16)
    acc_sc[...] = jnp.zeros(acc_sc.shape, F32)
    one = jnp.ones((1, t), F32)
    init = (jnp.full((1, t), NEG_BIG, F32), one, one)
    carry = lax.fori_loop(0, i, two_pairs, (init, init))
    carry = iteration(2 * i, 0, carry, diag=0)
    carry = iteration(2 * i + 1, 1, carry, diag=1, more=False)
    accumulate(2 * i + 1, carry, p_sets[1])
    outs = []
    for hh in range(2):
        acc = acc_sc[hh]
        outs.append(acc[:MLA_V] / acc[MLA_V:MLA_V + 1])
    o_ref[0] = jnp.concatenate(outs, axis=0).T.astype(BF16)


def _mla_attention(qm, km, vt, t):
    B, H, S, _ = qm.shape
    tv = vt.shape[-1]
    h = tv // 2
    assert t % tv == 0
    return pl.pallas_call(
        functools.partial(_mla_kernel, t=t, h=h),
        out_shape=jax.ShapeDtypeStruct((B, S, H * MLA_V), BF16),
        grid=(B, H // 2, S // t),
        in_specs=[
            pl.BlockSpec((1, 2, t, LANES), lambda b, hp, i: (b, hp, i, 0)),
            pl.BlockSpec((1, 2, S, LANES), lambda b, hp, i: (b, hp, 0, 0)),
            pl.BlockSpec((1, 1, S // tv, 2 * MLA_VX, tv), lambda b, hp, i: (b, hp, 0, 0, 0)),
        ],
        out_specs=pl.BlockSpec((1, t, LANES), lambda b, hp, i: (b, i, hp)),
        scratch_shapes=([pltpu.VMEM((2, h, t), F32)] * 4 + [pltpu.VMEM((2, h, t), BF16)] * 4
                        + [pltpu.VMEM((2, MLA_VX, t), F32), pltpu.VMEM((8, 8, t), F32)]),
        compiler_params=_cparams(("arbitrary", "arbitrary", "arbitrary")),
        name="mla_flash",
    )(qm, km, vt)


def _swa_kernel(sink_ref, q_ref, kc_ref, kp_ref, vc_ref, vp_ref, o_ref, s_sc, p_sc, *, tq):
    W = SWA_WINDOW
    G = SWA_HEADS // SWA_KV_HEADS
    i = pl.program_id(1)
    nsub = tq // W
    r = lax.broadcasted_iota(jnp.int32, (G * W, 2 * W), 0) % W
    c = lax.broadcasted_iota(jnp.int32, (G * W, 2 * W), 1)
    band = (c > r) & (c <= r + W)
    first = band & ((i > 0) | (c >= W))
    lane = lax.broadcasted_iota(jnp.int32, (W, LANES), 1)
    for kv in range(SWA_KV_HEADS):
        kcat = jnp.concatenate([kp_ref[0, kv], kc_ref[0, kv]], axis=0)
        for n in range(nsub):
            q4 = jnp.concatenate([q_ref[0, kv * G + g, n * W:(n + 1) * W, :] for g in range(G)], axis=0)
            s_sc[kv * nsub + n] = _dot_nt(q4, kcat[n * W:(n + 2) * W])
    for kv in range(SWA_KV_HEADS):
        sink = jnp.concatenate(
            [jnp.full((W, 1), sink_ref[kv * G + g], F32) for g in range(G)], axis=0)
        for n in range(nsub):
            s = jnp.where(first if n == 0 else band, s_sc[kv * nsub + n], NEG_BIG)
            m = jnp.maximum(jnp.max(s, axis=-1, keepdims=True), sink)
            p = jnp.exp(s - m)
            den = jnp.sum(p, axis=-1, keepdims=True) + jnp.exp(sink - m)
            p_sc[kv * nsub + n] = (p / den).astype(BF16)
    va = jnp.concatenate([vp_ref[0, 0], vc_ref[0, 0]], axis=0)
    vb = jnp.concatenate([vp_ref[0, 1], vc_ref[0, 1]], axis=0)
    for kv in range(SWA_KV_HEADS):
        v_even, v_odd = (va, vb) if kv == 0 else (vb, va)
        for n in range(nsub):
            for pr in range(G // 2):
                oe = _dot(p_sc[kv * nsub + n, (2 * pr) * W:(2 * pr + 1) * W, :], v_even[n * W:(n + 2) * W])
                oo = _dot(p_sc[kv * nsub + n, (2 * pr + 1) * W:(2 * pr + 2) * W, :], v_odd[n * W:(n + 2) * W])
                blk = kv * (G // 2) + pr
                o_ref[0, n * W:(n + 1) * W, blk * LANES:(blk + 1) * LANES] = (
                    jnp.where(lane < SWA_HD, oe, oo).astype(BF16))


def _swa_attention(sinks, qs, ks, vs, tq):
    B, _, S, _ = qs.shape
    W = SWA_WINDOW
    nb = tq // W
    cur = lambda nh: pl.BlockSpec((1, nh, tq, LANES), lambda b, i: (b, 0, i, 0))
    prev = lambda nh: pl.BlockSpec((1, nh, W, LANES), lambda b, i: (b, 0, jnp.maximum(i * nb - 1, 0), 0))
    return pl.pallas_call(
        functools.partial(_swa_kernel, tq=tq),
        out_shape=jax.ShapeDtypeStruct((B, S, SWA_HEADS * SWA_HD), BF16),
        grid=(B, S // tq),
        in_specs=[
            pl.BlockSpec(memory_space=pltpu.SMEM),
            cur(SWA_HEADS), cur(SWA_KV_HEADS), prev(SWA_KV_HEADS), cur(2), prev(2),
        ],
        out_specs=pl.BlockSpec((1, tq, SWA_HEADS * SWA_HD), lambda b, i: (b, i, 0)),
        scratch_shapes=[pltpu.VMEM((SWA_KV_HEADS * nb, SWA_HEADS // SWA_KV_HEADS * W, 2 * W), F32),
                        pltpu.VMEM((SWA_KV_HEADS * nb, SWA_HEADS // SWA_KV_HEADS * W, 2 * W), BF16)],
        compiler_params=_cparams(("arbitrary", "arbitrary")),
        name="swa_attn",
    )(sinks, qs, ks, ks, vs, vs)


def _mlp_tail(x, y, mod, w1_ref, w2_ref):
    x1 = _layer_norm(DN_ALPHA * x + (1.0 + mod[0:1, :]) * y, mod[4:5, :], mod[5:6, :])
    h = (x1 * (1.0 + mod[2:3, :]) + mod[1:2, :]).astype(BF16)
    acc = None
    for cidx in range(D_FF // FF_CHUNK):
        hid = _dot(h, w1_ref[:, cidx * FF_CHUNK:(cidx + 1) * FF_CHUNK])
        hid = jnp.square(jnp.maximum(hid, 0.0)).astype(BF16)
        part = _dot(hid, w2_ref[cidx * FF_CHUNK:(cidx + 1) * FF_CHUNK, :])
        acc = part if acc is None else acc + part
    return _layer_norm(DN_ALPHA * x1 + (1.0 + mod[3:4, :]) * acc, mod[6:7, :], mod[7:8, :])


def _att_post_kernel(x_ref, oa_ref, ob_ref, mod_ref, wout_ref, w1_ref, w2_ref, o_ref):
    half = wout_ref.shape[0] // 2
    y = _dot(oa_ref[0], wout_ref[:half, :]) + _dot(ob_ref[0], wout_ref[half:, :])
    o_ref[0] = _mlp_tail(x_ref[0], y, mod_ref[0], w1_ref, w2_ref)


def _rec_post_kernel(x_ref, oc_ref, ys_ref, u_ref, mod_ref, d_ref, gw_ref, gb_ref,
                     wout_ref, w1_ref, w2_ref, o_ref):
    half = wout_ref.shape[0] // 2
    yy = jax.nn.gelu(ys_ref[0] + d_ref[...] * u_ref[0])
    od = yy * jax.nn.sigmoid(_dot(yy.astype(BF16), gw_ref[...]) + gb_ref[...])
    y = _dot(oc_ref[0], wout_ref[:half, :]) + _dot(od.astype(BF16), wout_ref[half:, :])
    o_ref[0] = _mlp_tail(x_ref[0], y, mod_ref[0], w1_ref, w2_ref)


def _post(kernel, x, toks, mod, params, tm, name):
    B, S, D = x.shape
    tok = lambda a: pl.BlockSpec((1, tm, a.shape[-1]), lambda b, i: (b, i, 0))
    return pl.pallas_call(
        kernel,
        out_shape=jax.ShapeDtypeStruct((B, S, D), F32),
        grid=(B, S // tm),
        in_specs=([tok(x)] + [tok(a) for a in toks]
                  + [pl.BlockSpec((1, 8, D), lambda b, i: (b, 0, 0))]
                  + [_layer_spec(a, lyr) for a, lyr in params]),
        out_specs=tok(x),
        compiler_params=_cparams(("arbitrary", "arbitrary")),
        name=name,
    )(x, *toks, mod, *[a for a, _ in params])


def _rec_in_kernel(x_ref, mod_ref, win_ref, tab_ref, q_ref, k_ref, v_ref, g_ref, u_ref):
    mod = mod_ref[0]
    h = (x_ref[0] * (1.0 + mod[1:2, :]) + mod[0:1, :]).astype(BF16)
    z = _dot(h, win_ref[...])
    tab = tab_ref[...]
    t = lambda k: tab[:, k * LANES:(k + 1) * LANES]
    W = RET_HEADS * RET_DK
    for hd in range(RET_HEADS):
        sl = slice(hd * LANES, (hd + 1) * LANES)
        q_ref[0, :, sl] = _rope(z[:, hd * LANES:(hd + 1) * LANES], t(0), t(1)).astype(BF16)
        k_ref[0, :, sl] = _rope(z[:, W + hd * LANES:W + (hd + 1) * LANES], t(0), t(1)) * RET_DK ** -0.5
    v_ref[0] = z[:, 2 * W:3 * W].astype(BF16)
    g_ref[0] = z[:, 3 * W:4 * W]
    u_ref[0] = z[:, 4 * W:]


def _rec_in(x, mod, lyr, w_in, tab, tm):
    B, S, D = x.shape
    W = RET_HEADS * RET_DK
    sd = lambda dt: jax.ShapeDtypeStruct((B, S, W), dt)
    tok = pl.BlockSpec((1, tm, W), lambda b, i: (b, i, 0))
    return pl.pallas_call(
        _rec_in_kernel,
        out_shape=(sd(BF16), sd(F32), sd(BF16), sd(F32), sd(F32)),
        grid=(B, S // tm),
        in_specs=[
            pl.BlockSpec((1, tm, D), lambda b, i: (b, i, 0)),
            pl.BlockSpec((1, 8, D), lambda b, i: (b, 0, 0)),
            _layer_spec(w_in, lyr),
            pl.BlockSpec((tm, tab.shape[1]), lambda b, i: (i, 0)),
        ],
        out_specs=(tok, tok, tok, tok, tok),
        compiler_params=_cparams(("arbitrary", "arbitrary")),
        name="rec_in_proj",
    )(x, mod, w_in, tab)


def _ret_kernel(q_ref, k_ref, v_ref, g_ref, dec_ref, te_ref, fs_ref, cd_ref, o_ref,
                st_sc, sc_sc, upd_sc, sin_sc, *, tm):
    C = RET_CHUNK
    nch = tm // C

    @pl.when(pl.program_id(1) == 0)
    def _():
        st_sc[...] = jnp.zeros(st_sc.shape, F32)

    def operands(hd, n):
        sl = slice(hd * LANES, (hd + 1) * LANES)
        rows = slice(n * C, (n + 1) * C)
        return rows, sl

    for hd in range(RET_HEADS):
        for n in range(nch):
            rows, sl = operands(hd, n)
            k = k_ref[0, rows, sl]
            v = v_ref[0, rows, sl]
            sc_sc[hd * nch + n] = (_dot_nt(q_ref[0, rows, sl], k.astype(BF16)) * dec_ref[hd]).astype(BF16)
            upd_sc[hd * nch + n] = _dot_tn((k * te_ref[hd]).astype(BF16), v)
    for hd in range(RET_HEADS):
        state = st_sc[hd]
        for n in range(nch):
            sin_sc[hd * nch + n] = state.astype(BF16)
            state = cd_ref[hd] * state + upd_sc[hd * nch + n]
        st_sc[hd] = state
    for hd in range(RET_HEADS):
        for n in range(nch):
            rows, sl = operands(hd, n)
            o = (_dot(sc_sc[hd * nch + n], v_ref[0, rows, sl])
                 + _dot(q_ref[0, rows, sl], sin_sc[hd * nch + n]) * fs_ref[hd])
            mu = jnp.mean(o, axis=-1, keepdims=True)
            d = o - mu
            var = jnp.mean(d * d, axis=-1, keepdims=True)
            o = d * lax.rsqrt(var + LN_EPS)
            o_ref[0, rows, sl] = (jax.nn.silu(g_ref[0, rows, sl]) * o).astype(BF16)


def _retention(q, k, v, g, tm):
    B, S, W = q.shape
    H, C = RET_HEADS, RET_CHUNK
    log_gamma = jnp.log(1.0 - 2.0 ** (-5.0 - jnp.arange(H, dtype=F32)))
    idx = jnp.arange(C, dtype=F32)
    diff = idx[:, None] - idx[None, :]
    decay = jnp.where(diff >= 0, jnp.exp(log_gamma[:, None, None] * jnp.maximum(diff, 0.0)), 0.0)
    to_end = jnp.exp(log_gamma[:, None] * (C - 1.0 - idx)[None, :])
    from_start = jnp.exp((idx + 1.0)[None, :] * log_gamma[:, None])
    chunk_decay = jnp.exp(log_gamma * C)
    bc = lambda a: jnp.broadcast_to(a[:, :, None], (H, C, LANES))
    cd = jnp.broadcast_to(chunk_decay[:, None, None], (H, RET_DK, RET_DV))
    tok = pl.BlockSpec((1, tm, W), lambda b, i: (b, i, 0))
    tbl = _const_spec((H, C, LANES))
    return pl.pallas_call(
        functools.partial(_ret_kernel, tm=tm),
        out_shape=jax.ShapeDtypeStruct((B, S, W), BF16),
        grid=(B, S // tm),
        in_specs=[tok, tok, tok, tok, tbl, tbl, tbl, tbl],
        out_specs=tok,
        scratch_shapes=[pltpu.VMEM((H, RET_DK, RET_DV), F32),
                        pltpu.VMEM((H * tm // C, C, C), BF16),
                        pltpu.VMEM((H * tm // C, RET_DK, RET_DV), F32),
                        pltpu.VMEM((H * tm // C, RET_DK, RET_DV), BF16)],
        compiler_params=_cparams(("arbitrary", "arbitrary")),
        name="retention",
    )(q, k, v, g, decay, bc(to_end), bc(from_start), cd)


def _s5_kernel(u_ref, kern_ref, we_ref, wc_ref, mc_ref, ms_ref, y_ref, toep_sc, *, nchunk, nsteps):
    L = S5_CHUNK
    s_i = lax.broadcasted_iota(jnp.int32, (L, L), 0)
    t_i = lax.broadcasted_iota(jnp.int32, (L, L), 1)
    causal = t_i >= s_i
    for q in range(S5_GROUP):
        taps = kern_ref[0, q]
        for p in range(S5_GROUP):
            blk = pltpu.roll(jnp.broadcast_to(taps[p:p + 1, :], (L, L)), 0, 1, stride=1, stride_axis=0)
            toep_sc[q * L:(q + 1) * L, p * L:(p + 1) * L] = jnp.where(causal, blk, 0.0).astype(BF16)
    u = u_ref[0]
    R = u.shape[0]
    y = _dot(u, toep_sc[...])
    x = _dot(u, we_ref[0])
    cidx = lax.broadcasted_iota(jnp.int32, (R, LANES), 0) % nchunk
    mc = mc_ref[0]
    ms = ms_ref[0]
    for kk in range(nsteps):
        sh = 1 << kk
        xs = jnp.where(cidx >= sh, pltpu.roll(x, sh, 0), 0.0)
        x = x + xs * mc[kk:kk + 1, :] + pltpu.roll(xs, 64, 1) * ms[kk:kk + 1, :]
    xprev = jnp.where(cidx >= 1, pltpu.roll(x, 1, 0), 0.0)
    y_ref[0] = y + _dot(xprev.astype(BF16), wc_ref[0])


def _s5_prep(a_re, a_im, log_step, b_re, b_im, c_re, c_im, L, nsteps):
    G, N, P = S5_GROUPS, S5_STATE, S5_GROUP
    dt = jnp.exp(log_step)[:, None]
    lr, li = a_re, a_im
    la, th = lr * dt, li * dt
    mag = jnp.exp(la)
    ar, ai = mag * jnp.cos(th), mag * jnp.sin(th)
    den = lr * lr + li * li
    cr = ((ar - 1.0) * lr + ai * li) / den
    ci = (ai * lr - (ar - 1.0) * li) / den
    bbr = cr[..., None] * b_re - ci[..., None] * b_im
    bbi = cr[..., None] * b_im + ci[..., None] * b_re
    tau = jnp.arange(L + 1, dtype=F32)[:, None, None]
    pm = jnp.exp(la[None] * tau)
    pr, pi = pm * jnp.cos(th[None] * tau), pm * jnp.sin(th[None] * tau)
    car = c_re[None] * pr[:, :, None, :] - c_im[None] * pi[:, :, None, :]
    cai = c_re[None] * pi[:, :, None, :] + c_im[None] * pr[:, :, None, :]
    kern = (jnp.einsum('tgpn,gnq->gqpt', car[:L], bbr, precision=HI)
            - jnp.einsum('tgpn,gnq->gqpt', cai[:L], bbi, precision=HI))
    rev_r, rev_i = pr[L - 1::-1][:L], pi[L - 1::-1][:L]
    we_r = rev_r[..., None] * bbr[None] - rev_i[..., None] * bbi[None]
    we_i = rev_r[..., None] * bbi[None] + rev_i[..., None] * bbr[None]
    we = jnp.concatenate([we_r, we_i], axis=2).transpose(1, 3, 0, 2).reshape(G, P * L, 2 * N)
    wc = jnp.concatenate([car[1:], -cai[1:]], axis=3)
    wc = wc.transpose(1, 3, 2, 0).reshape(G, 2 * N, P * L)
    zr, zi = pr[L], pi[L]
    mcs, mss = [], []
    for _ in range(nsteps):
        mcs.append(jnp.concatenate([zr, zr], axis=-1))
        mss.append(jnp.concatenate([-zi, zi], axis=-1))
        zr, zi = zr * zr - zi * zi, 2.0 * zr * zi
    mc = jnp.stack(mcs, axis=1)
    ms = jnp.stack(mss, axis=1)
    return kern, we.astype(BF16), wc.astype(BF16), mc, ms


def _s5_scan(u, a_re, a_im, log_step, b_re, b_im, c_re, c_im):
    B, S, W = u.shape
    G, P, N, L = S5_GROUPS, S5_GROUP, S5_STATE, S5_CHUNK
    nchunk = S // L
    nsteps = max(1, (nchunk - 1).bit_length())
    kern, we, wc, mc, ms = _s5_prep(a_re, a_im, log_step, b_re, b_im, c_re, c_im, L, nsteps)
    R = B * nchunk
    ug = u.astype(BF16).reshape(B, nchunk, L, G, P).transpose(3, 0, 1, 4, 2).reshape(G, R, P * L)
    grp = lambda a: pl.BlockSpec((1,) + a.shape[1:], lambda g: (g,) + (0,) * (a.ndim - 1))
    y = pl.pallas_call(
        functools.partial(_s5_kernel, nchunk=nchunk, nsteps=nsteps),
        out_shape=jax.ShapeDtypeStruct((G, R, P * L), F32),
        grid=(G,),
        in_specs=[grp(ug), grp(kern), grp(we), grp(wc), grp(mc), grp(ms)],
        out_specs=pl.BlockSpec((1, R, P * L), lambda g: (g, 0, 0)),
        scratch_shapes=[pltpu.VMEM((P * L, P * L), BF16)],
        compiler_params=_cparams(("arbitrary",)),
        name="s5_conv",
    )(ug, kern, we, wc, mc, ms)
    return y.reshape(G, B, nchunk, P, L).transpose(1, 2, 4, 0, 3).reshape(B, S, W)


def _pick_tile(S, pref):
    t = min(pref, S)
    assert S % t == 0, (S, t)
    return t


def kernel(x, c, ada_w, ada_b, ln_g, ln_b, att_w_in, mla_q_norm, mla_w_uq, mla_kv_norm, mla_w_ukv, swa_sinks, att_w_out, rec_w_in, s5_a_re, s5_a_im, s5_log_step, s5_b_re, s5_b_im, s5_c_re, s5_c_im, s5_d, s5_glu_w, s5_glu_b, rec_w_out, mlp_w1, mlp_w2):
    B, S, D = x.shape
    assert D == D_MODEL and S % SWA_WINDOW == 0 and S % S5_CHUNK == 0
    tm = _pick_tile(S, 512)

    mods = _modulation(c, ada_w, ada_b)
    zeros = jnp.zeros((B, D), F32)

    def in_mod(m):
        rows = [m[:, :D], m[:, D:2 * D]] + [zeros] * 6
        return jnp.stack(rows, axis=1)

    def post_mod(l, m1, m2):
        bc = lambda v: jnp.broadcast_to(v[None, :], (B, D))
        rows = [m1[:, 2 * D:], m2[:, :D], m2[:, D:2 * D], m2[:, 2 * D:],
                bc(ln_g[l, 0]), bc(ln_b[l, 0]), bc(ln_g[l, 1]), bc(ln_b[l, 1])]
        return jnp.stack(rows, axis=1)

    src_in, src_uq, src_uk, src_uv = _att_layouts()
    att_tab = jnp.concatenate(_rope_pieces(S, MLA_ROPE) + _rope_pieces(S, SWA_HD), axis=1)
    rec_tab = jnp.concatenate(_rope_pieces(S, RET_DK), axis=1)

    w_in_att = _place_cols(att_w_in, src_in).astype(BF16)
    w_uq = _place_cols(mla_w_uq, src_uq).astype(BF16)
    w_uk = _place_cols(mla_w_ukv, src_uk).astype(BF16)
    w_uv = _place_cols(mla_w_ukv, src_uv).astype(BF16)
    q_norm = mla_q_norm[:, None, :]
    kv_norm = mla_kv_norm[:, None, :]
    w_out_att = att_w_out.astype(BF16)
    w_in_rec = rec_w_in.astype(BF16)
    w_out_rec = rec_w_out.astype(BF16)
    glu_w = s5_glu_w.astype(BF16)
    s5_dd = s5_d[:, None, :]
    glu_b = s5_glu_b[:, None, :]
    w1 = mlp_w1.astype(BF16)
    w2 = mlp_w2.astype(BF16)

    for l in range(DEPTH):
        j = l // 2
        m1, m2 = mods[2 * l], mods[2 * l + 1]
        pmod = post_mod(l, m1, m2)
        if l % 2 == 0:
            qm, km, vt, qs, ks, vs = _att_in(x, in_mod(m1), j, w_in_att, q_norm, w_uq, kv_norm, w_uk, w_uv,
                                             att_tab, tm)
            o_a = _mla_attention(qm, km, vt, _pick_tile(S, 1024))
            o_b = _swa_attention(swa_sinks[j], qs, ks, vs, tm)
            x = _post(_att_post_kernel, x, [o_a, o_b], pmod,
                      [(w_out_att, j), (w1, l), (w2, l)], tm, "att_post")
        else:
            rq, rk, rv, rg, u = _rec_in(x, in_mod(m1), j, w_in_rec, rec_tab, tm)
            o_c = _retention(rq, rk, rv, rg, tm)
            ys = _s5_scan(u, s5_a_re[j], s5_a_im[j], s5_log_step[j], s5_b_re[j], s5_b_im[j],
                          s5_c_re[j], s5_c_im[j])
            x = _post(_rec_post_kernel, x, [o_c, ys, u], pmod,
                      [(s5_dd, j), (glu_w, j), (glu_b, j), (w_out_rec, j), (w1, l), (w2, l)], tm, "rec_post")
    return x
```

```python
import functools
import math

import jax
import jax.numpy as jnp
import numpy as np
from jax import lax
from jax.experimental import pallas as pl
from jax.experimental.pallas import tpu as pltpu

F32 = jnp.float32
BF16 = jnp.bfloat16

D_MODEL = 1024
DEPTH = 4
DN_ALPHA = (2.0 * DEPTH) ** 0.25
LN_EPS = 1e-5
RMS_EPS = 1e-6
ROPE_THETA = 10000.0
MLA_HEADS = 8
MLA_NOPE = 64
MLA_ROPE = 32
MLA_V = 64
MLA_Q_RANK = 384
MLA_KV_RANK = 256
SWA_HEADS = 8
SWA_KV_HEADS = 2
SWA_HD = 64
SWA_WINDOW = 128
RET_HEADS = 4
RET_DK = 128
RET_DV = 128
RET_CHUNK = 128
S5_WIDTH = 512
S5_GROUP = 16
S5_GROUPS = 32
S5_STATE = 64
S5_CHUNK = 128
D_FF = 4096
FF_CHUNK = 1024

LANES = 128
VMEM_LIMIT = 56 * 1024 * 1024
NEG_BIG = -1e30
HI = lax.Precision.HIGHEST
MLA_VX = MLA_V + 16
MLA_Q_SCALE = (MLA_NOPE + MLA_ROPE) ** -0.5 * math.log2(math.e)


def _cparams(sem):
    return pltpu.CompilerParams(dimension_semantics=sem, vmem_limit_bytes=VMEM_LIMIT)


def _const_spec(shape):
    nd = len(shape)
    return pl.BlockSpec(shape, lambda *_: (0,) * nd, pipeline_mode=pl.Buffered(1))


def _layer_spec(arr, lyr):
    return pl.BlockSpec((None,) + arr.shape[1:], lambda *_: (lyr, 0, 0), pipeline_mode=pl.Buffered(1))


def _dot(a, b):
    return jnp.dot(a, b, preferred_element_type=F32)


def _dot_nt(a, b):
    return lax.dot_general(a, b, (((1,), (1,)), ((), ())), preferred_element_type=F32)


def _dot_tn(a, b):
    return lax.dot_general(a, b, (((0,), (0,)), ((), ())), preferred_element_type=F32)


def _rope(x, c, s):
    return x * c + pltpu.roll(x, 64, 1) * s


def _layer_norm(v, g, b):
    mu = jnp.mean(v, axis=-1, keepdims=True)
    d = v - mu
    var = jnp.mean(d * d, axis=-1, keepdims=True)
    return d * lax.rsqrt(var + LN_EPS) * g + b


def _mod_kernel(c_ref, w_ref, b_ref, o_ref):
    cond = jax.nn.silu(c_ref[...]).astype(BF16)
    o_ref[0] = _dot(cond, w_ref[0].astype(BF16)) + b_ref[0]


def _modulation(c, ada_w, ada_b):
    B = c.shape[0]
    nmod = ada_w.shape[0] * ada_w.shape[1]
    rows = 8
    tn = 768
    c8 = jnp.zeros((rows, D_MODEL), F32).at[:B].set(c)
    w = ada_w.reshape(nmod, D_MODEL, 3 * D_MODEL)
    b = ada_b.reshape(nmod, 1, 3 * D_MODEL)
    out = pl.pallas_call(
        _mod_kernel,
        out_shape=jax.ShapeDtypeStruct((nmod, rows, 3 * D_MODEL), F32),
        grid=(nmod, 3 * D_MODEL // tn),
        in_specs=[
            pl.BlockSpec((rows, D_MODEL), lambda m, n: (0, 0)),
            pl.BlockSpec((1, D_MODEL, tn), lambda m, n: (m, 0, n)),
            pl.BlockSpec((1, 1, tn), lambda m, n: (m, 0, n)),
        ],
        out_specs=pl.BlockSpec((1, rows, tn), lambda m, n: (m, 0, n)),
        compiler_params=_cparams(("arbitrary", "arbitrary")),
        name="adaln_mod",
    )(c8, w, b)
    return out[:, :B, :]


def _rope_pieces(seq, dim):
    half = dim // 2
    inv = ROPE_THETA ** (-jnp.arange(0, dim, 2, dtype=F32) / dim)
    ang = jnp.arange(seq, dtype=F32)[:, None] * inv[None, :]
    cos = jnp.tile(jnp.cos(ang), (1, LANES // half))
    sin = jnp.tile(jnp.sin(ang), (1, LANES // half))
    lane = np.arange(LANES)
    lo = jnp.asarray(lane < half)
    hi = jnp.asarray((lane >= 64) & (lane < 64 + half))
    return [jnp.where(lo | hi, cos, 1.0), jnp.where(lo, -sin, jnp.where(hi, sin, 0.0))]


def _place_cols(w, src):
    src = np.asarray(src)
    pieces, start = [], 0
    while start < len(src):
        end = start + 1
        while end < len(src) and ((src[start] < 0 and src[end] < 0)
                                  or (src[start] >= 0 and src[end] == src[end - 1] + 1)):
            end += 1
        if src[start] < 0:
            pieces.append(jnp.zeros(w.shape[:-1] + (end - start,), w.dtype))
        else:
            pieces.append(w[..., src[start]:src[start] + end - start])
        start = end
    return jnp.concatenate(pieces, axis=-1)


def _att_layouts():
    off_kr = MLA_Q_RANK + MLA_KV_RANK
    off_sq = off_kr + MLA_ROPE
    off_sk = off_sq + SWA_HEADS * SWA_HD
    off_sv = off_sk + SWA_KV_HEADS * SWA_HD
    src = list(range(off_kr))
    grp = [-1] * LANES
    for i in range(16):
        grp[i] = off_kr + i
        grp[64 + i] = off_kr + 16 + i
    src += grp
    for base, nh in ((off_sq, SWA_HEADS), (off_sk, SWA_KV_HEADS)):
        for h in range(nh):
            grp = [-1] * LANES
            for i in range(32):
                grp[i] = base + h * SWA_HD + i
                grp[64 + i] = base + h * SWA_HD + 32 + i
            src += grp
    src += list(range(off_sv, off_sv + SWA_KV_HEADS * SWA_HD))
    uq, uk, uv = [], [], []
    for h in range(MLA_HEADS):
        qb = h * (MLA_NOPE + MLA_ROPE)
        kb = h * (MLA_NOPE + MLA_V)
        gq = [-1] * LANES
        gk = [-1] * LANES
        for i in range(16):
            gq[i] = qb + MLA_NOPE + i
            gq[64 + i] = qb + MLA_NOPE + 16 + i
        for i in range(48):
            gq[16 + i] = qb + i
            gk[16 + i] = kb + i
        for i in range(16):
            gq[80 + i] = qb + 48 + i
            gk[80 + i] = kb + 48 + i
        uq += gq
        uk += gk
        uv += list(range(kb + MLA_NOPE, kb + MLA_NOPE + MLA_V))
    return src, uq, uk, uv


def _att_in_kernel(x_ref, mod_ref, win_ref, qn_ref, wuq_ref, kvn_ref, wuk_ref, wuv_ref, tab_ref,
                   qm_ref, km_ref, vm_ref, qs_ref, ks_ref, vs_ref):
    mod = mod_ref[0]
    h = (x_ref[0] * (1.0 + mod[1:2, :]) + mod[0:1, :]).astype(BF16)
    z = _dot(h, win_ref[...])
    tab = tab_ref[...]
    t = lambda k: tab[:, k * LANES:(k + 1) * LANES]

    c_q = z[:, :MLA_Q_RANK]
    c_q = c_q * lax.rsqrt(jnp.mean(c_q * c_q, axis=-1, keepdims=True) + RMS_EPS) * qn_ref[...]
    q = _dot(c_q.astype(BF16), wuq_ref[...])
    c_kv = z[:, MLA_Q_RANK:MLA_Q_RANK + MLA_KV_RANK]
    c_kv = c_kv * lax.rsqrt(jnp.mean(c_kv * c_kv, axis=-1, keepdims=True) + RMS_EPS) * kvn_ref[...]
    c_kv = c_kv.astype(BF16)
    kup = _dot(c_kv, wuk_ref[...])
    v = _dot(c_kv, wuv_ref[...])
    off = MLA_Q_RANK + MLA_KV_RANK
    kr = _rope(z[:, off:off + LANES], t(0), t(1))
    for hd in range(MLA_HEADS):
        sl = slice(hd * LANES, (hd + 1) * LANES)
        qm_ref[0, hd] = (_rope(q[:, sl], t(0), t(1)) * MLA_Q_SCALE).astype(BF16)
        km_ref[0, hd] = (kup[:, sl] + kr).astype(BF16)
    tm = v.shape[0]
    ones_row = jnp.where(lax.broadcasted_iota(jnp.int32, (MLA_VX - MLA_V, tm), 0) == 0, 1.0, 0.0)
    for hp in range(MLA_HEADS // 2):
        vt = v[:, hp * LANES:(hp + 1) * LANES].T
        vm_ref[0, hp, 0] = jnp.concatenate(
            [vt[:MLA_V], ones_row, vt[MLA_V:], ones_row], axis=0).astype(BF16)
    off += LANES
    for hd in range(SWA_HEADS):
        qs_ref[0, hd] = (_rope(z[:, off + hd * LANES:off + (hd + 1) * LANES], t(2), t(3))
                         * SWA_HD ** -0.5).astype(BF16)
    off += SWA_HEADS * LANES
    for kv in range(SWA_KV_HEADS):
        ks_ref[0, kv] = _rope(z[:, off + kv * LANES:off + (kv + 1) * LANES], t(2), t(3)).astype(BF16)
    off += SWA_KV_HEADS * LANES
    sv = z[:, off:off + LANES]
    vs_ref[0, 0] = sv.astype(BF16)
    vs_ref[0, 1] = pltpu.roll(sv, 64, 1).astype(BF16)


def _att_in(x, mod, lyr, w_in_p, q_norm, w_uq_p, kv_norm, w_uk_p, w_uv_p, tab, tm):
    B, S, D = x.shape
    bf = lambda *s: jax.ShapeDtypeStruct(s, BF16)
    tok4 = lambda nh: pl.BlockSpec((1, nh, tm, LANES), lambda b, i: (b, 0, i, 0))
    return pl.pallas_call(
        _att_in_kernel,
        out_shape=(bf(B, MLA_HEADS, S, LANES), bf(B, MLA_HEADS, S, LANES),
                   bf(B, MLA_HEADS // 2, S // tm, 2 * MLA_VX, tm),
                   bf(B, SWA_HEADS, S, LANES), bf(B, SWA_KV_HEADS, S, LANES), bf(B, 2, S, LANES)),
        grid=(B, S // tm),
        in_specs=[
            pl.BlockSpec((1, tm, D), lambda b, i: (b, i, 0)),
            pl.BlockSpec((1, 8, D), lambda b, i: (b, 0, 0)),
            _layer_spec(w_in_p, lyr), _layer_spec(q_norm, lyr), _layer_spec(w_uq_p, lyr),
            _layer_spec(kv_norm, lyr), _layer_spec(w_uk_p, lyr), _layer_spec(w_uv_p, lyr),
            pl.BlockSpec((tm, tab.shape[1]), lambda b, i: (i, 0)),
        ],
        out_specs=(tok4(MLA_HEADS), tok4(MLA_HEADS),
                   pl.BlockSpec((1, MLA_HEADS // 2, 1, 2 * MLA_VX, tm), lambda b, i: (b, 0, i, 0, 0)),
                   tok4(SWA_HEADS), tok4(SWA_KV_HEADS), tok4(2)),
        compiler_params=_cparams(("arbitrary", "arbitrary")),
        name="att_in_proj",
    )(x, mod, w_in_p, q_norm, w_uq_p, kv_norm, w_uk_p, w_uv_p, tab)


def _mla_kernel(q_ref, k_ref, vt_ref, o_ref, st0a, st0b, st1a, st1b, p0a, p0b, p1a, p1b, acc_sc, cm_sc,
                *, t, h):
    i = pl.program_id(2)
    assert t == 4 * h
    st_sets = ((st0a, st0b), (st1a, st1b))
    p_sets = ((p0a, p0b), (p1a, p1b))

    def scores(blk, st_set, half):
        k0 = pl.multiple_of(blk * h, h)
        for hh in range(2):
            st = _dot_nt(k_ref[0, hh, pl.ds(k0, h), :], q_ref[0, hh])
            st_sets[st_set][half][hh] = st
            cm_sc[(st_set * 2 + half) * 2 + hh, 0:1, :] = jnp.max(st, axis=0, keepdims=True)

    def values(vt_blk, half, p_rd, hh):
        vt = vt_ref[0, 0, vt_blk, hh * MLA_VX:(hh + 1) * MLA_VX, half * h:(half + 1) * h]
        return _dot(vt, p_rd[hh])

    def accumulate(vt_blk, carry, p_rd):
        for hh in range(2):
            _, alpha_a, alpha_b = carry[hh]
            acc = alpha_a * acc_sc[hh] + values(vt_blk, 0, p_rd[0], hh)
            acc_sc[hh] = alpha_b * acc + values(vt_blk, 1, p_rd[1], hh)

    def softmax(carry, par, diag):
        out = []
        for hh in range(2):
            m = carry[hh][0]
            alphas = []
            for half in range(2):
                st = st_sets[par][half][hh]
                if diag is None:
                    blk_max = cm_sc[(par * 2 + half) * 2 + hh, 0:1, :]
                else:
                    key = lax.broadcasted_iota(jnp.int32, (h, t), 0) + (2 * diag + half) * h
                    qry = lax.broadcasted_iota(jnp.int32, (h, t), 1)
                    st = jnp.where(key <= qry, st, NEG_BIG)
                    blk_max = jnp.max(st, axis=0, keepdims=True)
                m_new = jnp.maximum(m, blk_max)
                alphas.append(jnp.exp2(m - m_new))
                p_sets[par][half][hh] = jnp.exp2(st - m_new).astype(BF16)
                m = m_new
            out.append((m, alphas[0], alphas[1]))
        return tuple(out)

    def iteration(c, par, carry, diag=None, more=True):
        if more:
            scores(2 * c + 2, 1 - par, 0)
            scores(2 * c + 3, 1 - par, 1)
        accumulate(jnp.maximum(c - 1, 0), carry, p_sets[1 - par])
        return softmax(carry, par, diag)

    def two_pairs(cc, carry):
        carry = iteration(2 * cc, 0, carry)
        return iteration(2 * cc + 1, 1, carry)

    scores(0, 0, 0)
    scores(1, 0, 1)
    p1a[...] = jnp.zeros(p1a.shape, BF16)
    p1b[...] = jnp.zeros(p1b.shape, BF16)
    acc_sc[...] = jnp.zeros(acc_sc.shape, F32)
    one = jnp.ones((1, t), F32)
    init = (jnp.full((1, t), NEG_BIG, F32), one, one)
    carry = lax.fori_loop(0, i, two_pairs, (init, init))
    carry = iteration(2 * i, 0, carry, diag=0)
    carry = iteration(2 * i + 1, 1, carry, diag=1, more=False)
    accumulate(2 * i + 1, carry, p_sets[1])
    outs = []
    for hh in range(2):
        acc = acc_sc[hh]
        outs.append(acc[:MLA_V] / acc[MLA_V:MLA_V + 1])
    o_ref[0] = jnp.concatenate(outs, axis=0).T.astype(BF16)


def _mla_attention(qm, km, vt, t):
    B, H, S, _ = qm.shape
    tv = vt.shape[-1]
    h = tv // 2
    assert t % tv == 0
    return pl.pallas_call(
        functools.partial(_mla_kernel, t=t, h=h),
        out_shape=jax.ShapeDtypeStruct((B, S, H * MLA_V), BF16),
        grid=(B, H // 2, S // t),
        in_specs=[
            pl.BlockSpec((1, 2, t, LANES), lambda b, hp, i: (b, hp, i, 0)),
            pl.BlockSpec((1, 2, S, LANES), lambda b, hp, i: (b, hp, 0, 0)),
            pl.BlockSpec((1, 1, S // tv, 2 * MLA_VX, tv), lambda b, hp, i: (b, hp, 0, 0, 0)),
        ],
        out_specs=pl.BlockSpec((1, t, LANES), lambda b, hp, i: (b, i, hp)),
        scratch_shapes=([pltpu.VMEM((2, h, t), F32)] * 4 + [pltpu.VMEM((2, h, t), BF16)] * 4
                        + [pltpu.VMEM((2, MLA_VX, t), F32), pltpu.VMEM((8, 8, t), F32)]),
        compiler_params=_cparams(("arbitrary", "arbitrary", "arbitrary")),
        name="mla_flash",
    )(qm, km, vt)


def _swa_kernel(sink_ref, q_ref, kc_ref, kp_ref, vc_ref, vp_ref, o_ref, s_sc, p_sc, *, tq):
    W = SWA_WINDOW
    G = SWA_HEADS // SWA_KV_HEADS
    i = pl.program_id(1)
    nsub = tq // W
    r = lax.broadcasted_iota(jnp.int32, (G * W, 2 * W), 0) % W
    c = lax.broadcasted_iota(jnp.int32, (G * W, 2 * W), 1)
    band = (c > r) & (c <= r + W)
    first = band & ((i > 0) | (c >= W))
    lane = lax.broadcasted_iota(jnp.int32, (W, LANES), 1)
    for kv in range(SWA_KV_HEADS):
        kcat = jnp.concatenate([kp_ref[0, kv], kc_ref[0, kv]], axis=0)
        for n in range(nsub):
            q4 = jnp.concatenate([q_ref[0, kv * G + g, n * W:(n + 1) * W, :] for g in range(G)], axis=0)
            s_sc[kv * nsub + n] = _dot_nt(q4, kcat[n * W:(n + 2) * W])
    for kv in range(SWA_KV_HEADS):
        sink = jnp.concatenate(
            [jnp.full((W, 1), sink_ref[kv * G + g], F32) for g in range(G)], axis=0)
        for n in range(nsub):
            s = jnp.where(first if n == 0 else band, s_sc[kv * nsub + n], NEG_BIG)
            m = jnp.maximum(jnp.max(s, axis=-1, keepdims=True), sink)
            p = jnp.exp(s - m)
            den = jnp.sum(p, axis=-1, keepdims=True) + jnp.exp(sink - m)
            p_sc[kv * nsub + n] = (p / den).astype(BF16)
    va = jnp.concatenate([vp_ref[0, 0], vc_ref[0, 0]], axis=0)
    vb = jnp.concatenate([vp_ref[0, 1], vc_ref[0, 1]], axis=0)
    for kv in range(SWA_KV_HEADS):
        v_even, v_odd = (va, vb) if kv == 0 else (vb, va)
        for n in range(nsub):
            for pr in range(G // 2):
                oe = _dot(p_sc[kv * nsub + n, (2 * pr) * W:(2 * pr + 1) * W, :], v_even[n * W:(n + 2) * W])
                oo = _dot(p_sc[kv * nsub + n, (2 * pr + 1) * W:(2 * pr + 2) * W, :], v_odd[n * W:(n + 2) * W])
                blk = kv * (G // 2) + pr
                o_ref[0, n * W:(n + 1) * W, blk * LANES:(blk + 1) * LANES] = (
                    jnp.where(lane < SWA_HD, oe, oo).astype(BF16))


def _swa_attention(sinks, qs, ks, vs, tq):
    B, _, S, _ = qs.shape
    W = SWA_WINDOW
    nb = tq // W
    cur = lambda nh: pl.BlockSpec((1, nh, tq, LANES), lambda b, i: (b, 0, i, 0))
    prev = lambda nh: pl.BlockSpec((1, nh, W, LANES), lambda b, i: (b, 0, jnp.maximum(i * nb - 1, 0), 0))
    return pl.pallas_call(
        functools.partial(_swa_kernel, tq=tq),
        out_shape=jax.ShapeDtypeStruct((B, S, SWA_HEADS * SWA_HD), BF16),
        grid=(B, S // tq),
        in_specs=[
            pl.BlockSpec(memory_space=pltpu.SMEM),
            cur(SWA_HEADS), cur(SWA_KV_HEADS), prev(SWA_KV_HEADS), cur(2), prev(2),
        ],
        out_specs=pl.BlockSpec((1, tq, SWA_HEADS * SWA_HD), lambda b, i: (b, i, 0)),
        scratch_shapes=[pltpu.VMEM((SWA_KV_HEADS * nb, SWA_HEADS // SWA_KV_HEADS * W, 2 * W), F32),
                        pltpu.VMEM((SWA_KV_HEADS * nb, SWA_HEADS // SWA_KV_HEADS * W, 2 * W), BF16)],
        compiler_params=_cparams(("arbitrary", "arbitrary")),
        name="swa_attn",
    )(sinks, qs, ks, ks, vs, vs)


def _mlp_tail(x, y, mod, w1_ref, w2_ref):
    x1 = _layer_norm(DN_ALPHA * x + (1.0 + mod[0:1, :]) * y, mod[4:5, :], mod[5:6, :])
    h = (x1 * (1.0 + mod[2:3, :]) + mod[1:2, :]).astype(BF16)
    acc = None
    for cidx in range(D_FF // FF_CHUNK):
        hid = _dot(h, w1_ref[:, cidx * FF_CHUNK:(cidx + 1) * FF_CHUNK])
        hid = jnp.square(jnp.maximum(hid, 0.0)).astype(BF16)
        part = _dot(hid, w2_ref[cidx * FF_CHUNK:(cidx + 1) * FF_CHUNK, :])
        acc = part if acc is None else acc + part
    return _layer_norm(DN_ALPHA * x1 + (1.0 + mod[3:4, :]) * acc, mod[6:7, :], mod[7:8, :])


def _att_post_kernel(x_ref, oa_ref, ob_ref, mod_ref, wout_ref, w1_ref, w2_ref, o_ref):
    half = wout_ref.shape[0] // 2
    y = _dot(oa_ref[0], wout_ref[:half, :]) + _dot(ob_ref[0], wout_ref[half:, :])
    o_ref[0] = _mlp_tail(x_ref[0], y, mod_ref[0], w1_ref, w2_ref)


def _rec_post_kernel(x_ref, oc_ref, ys_ref, u_ref, mod_ref, d_ref, gw_ref, gb_ref,
                     wout_ref, w1_ref, w2_ref, o_ref):
    half = wout_ref.shape[0] // 2
    yy = jax.nn.gelu(ys_ref[0] + d_ref[...] * u_ref[0])
    od = yy * jax.nn.sigmoid(_dot(yy.astype(BF16), gw_ref[...]) + gb_ref[...])
    y = _dot(oc_ref[0], wout_ref[:half, :]) + _dot(od.astype(BF16), wout_ref[half:, :])
    o_ref[0] = _mlp_tail(x_ref[0], y, mod_ref[0], w1_ref, w2_ref)


def _post(kernel, x, toks, mod, params, tm, name):
    B, S, D = x.shape
    tok = lambda a: pl.BlockSpec((1, tm, a.shape[-1]), lambda b, i: (b, i, 0))
    return pl.pallas_call(
        kernel,
        out_shape=jax.ShapeDtypeStruct((B, S, D), F32),
        grid=(B, S // tm),
        in_specs=([tok(x)] + [tok(a) for a in toks]
                  + [pl.BlockSpec((1, 8, D), lambda b, i: (b, 0, 0))]
                  + [_layer_spec(a, lyr) for a, lyr in params]),
        out_specs=tok(x),
        compiler_params=_cparams(("arbitrary", "arbitrary")),
        name=name,
    )(x, *toks, mod, *[a for a, _ in params])


def _rec_in_kernel(x_ref, mod_ref, win_ref, tab_ref, q_ref, k_ref, v_ref, g_ref, u_ref):
    mod = mod_ref[0]
    h = (x_ref[0] * (1.0 + mod[1:2, :]) + mod[0:1, :]).astype(BF16)
    z = _dot(h, win_ref[...])
    tab = tab_ref[...]
    t = lambda k: tab[:, k * LANES:(k + 1) * LANES]
    W = RET_HEADS * RET_DK
    for hd in range(RET_HEADS):
        sl = slice(hd * LANES, (hd + 1) * LANES)
        q_ref[0, :, sl] = _rope(z[:, hd * LANES:(hd + 1) * LANES], t(0), t(1)).astype(BF16)
        k_ref[0, :, sl] = _rope(z[:, W + hd * LANES:W + (hd + 1) * LANES], t(0), t(1)) * RET_DK ** -0.5
    v_ref[0] = z[:, 2 * W:3 * W].astype(BF16)
    g_ref[0] = z[:, 3 * W:4 * W]
    u_ref[0] = z[:, 4 * W:]


def _rec_in(x, mod, lyr, w_in, tab, tm):
    B, S, D = x.shape
    W = RET_HEADS * RET_DK
    sd = lambda dt: jax.ShapeDtypeStruct((B, S, W), dt)
    tok = pl.BlockSpec((1, tm, W), lambda b, i: (b, i, 0))
    return pl.pallas_call(
        _rec_in_kernel,
        out_shape=(sd(BF16), sd(F32), sd(BF16), sd(F32), sd(F32)),
        grid=(B, S // tm),
        in_specs=[
            pl.BlockSpec((1, tm, D), lambda b, i: (b, i, 0)),
            pl.BlockSpec((1, 8, D), lambda b, i: (b, 0, 0)),
            _layer_spec(w_in, lyr),
            pl.BlockSpec((tm, tab.shape[1]), lambda b, i: (i, 0)),
        ],
        out_specs=(tok, tok, tok, tok, tok),
        compiler_params=_cparams(("arbitrary", "arbitrary")),
        name="rec_in_proj",
    )(x, mod, w_in, tab)


def _ret_kernel(q_ref, k_ref, v_ref, g_ref, dec_ref, te_ref, fs_ref, cd_ref, o_ref,
                st_sc, sc_sc, upd_sc, sin_sc, *, tm):
    C = RET_CHUNK
    nch = tm // C

    @pl.when(pl.program_id(1) == 0)
    def _():
        st_sc[...] = jnp.zeros(st_sc.shape, F32)

    def operands(hd, n):
        sl = slice(hd * LANES, (hd + 1) * LANES)
        rows = slice(n * C, (n + 1) * C)
        return rows, sl

    for hd in range(RET_HEADS):
        for n in range(nch):
            rows, sl = operands(hd, n)
            k = k_ref[0, rows, sl]
            v = v_ref[0, rows, sl]
            sc_sc[hd * nch + n] = (_dot_nt(q_ref[0, rows, sl], k.astype(BF16)) * dec_ref[hd]).astype(BF16)
            upd_sc[hd * nch + n] = _dot_tn((k * te_ref[hd]).astype(BF16), v)
    for hd in range(RET_HEADS):
        state = st_sc[hd]
        for n in range(nch):
            sin_sc[hd * nch + n] = state.astype(BF16)
            state = cd_ref[hd] * state + upd_sc[hd * nch + n]
        st_sc[hd] = state
    for hd in range(RET_HEADS):
        for n in range(nch):
            rows, sl = operands(hd, n)
            o = (_dot(sc_sc[hd * nch + n], v_ref[0, rows, sl])
                 + _dot(q_ref[0, rows, sl], sin_sc[hd * nch + n]) * fs_ref[hd])
            mu = jnp.mean(o, axis=-1, keepdims=True)
            d = o - mu
            var = jnp.mean(d * d, axis=-1, keepdims=True)
            o = d * lax.rsqrt(var + LN_EPS)
            o_ref[0, rows, sl] = (jax.nn.silu(g_ref[0, rows, sl]) * o).astype(BF16)


def _retention(q, k, v, g, tm):
    B, S, W = q.shape
    H, C = RET_HEADS, RET_CHUNK
    log_gamma = jnp.log(1.0 - 2.0 ** (-5.0 - jnp.arange(H, dtype=F32)))
    idx = jnp.arange(C, dtype=F32)
    diff = idx[:, None] - idx[None, :]
    decay = jnp.where(diff >= 0, jnp.exp(log_gamma[:, None, None] * jnp.maximum(diff, 0.0)), 0.0)
    to_end = jnp.exp(log_gamma[:, None] * (C - 1.0 - idx)[None, :])
    from_start = jnp.exp((idx + 1.0)[None, :] * log_gamma[:, None])
    chunk_decay = jnp.exp(log_gamma * C)
    bc = lambda a: jnp.broadcast_to(a[:, :, None], (H, C, LANES))
    cd = jnp.broadcast_to(chunk_decay[:, None, None], (H, RET_DK, RET_DV))
    tok = pl.BlockSpec((1, tm, W), lambda b, i: (b, i, 0))
    tbl = _const_spec((H, C, LANES))
    return pl.pallas_call(
        functools.partial(_ret_kernel, tm=tm),
        out_shape=jax.ShapeDtypeStruct((B, S, W), BF16),
        grid=(B, S // tm),
        in_specs=[tok, tok, tok, tok, tbl, tbl, tbl, tbl],
        out_specs=tok,
        scratch_shapes=[pltpu.VMEM((H, RET_DK, RET_DV), F32),
                        pltpu.VMEM((H * tm // C, C, C), BF16),
                        pltpu.VMEM((H * tm // C, RET_DK, RET_DV), F32),
                        pltpu.VMEM((H * tm // C, RET_DK, RET_DV), BF16)],
        compiler_params=_cparams(("arbitrary", "arbitrary")),
        name="retention",
    )(q, k, v, g, decay, bc(to_end), bc(from_start), cd)


def _s5_kernel(u_ref, kern_ref, we_ref, wc_ref, mc_ref, ms_ref, y_ref, toep_sc, *, nchunk, nsteps):
    L = S5_CHUNK
    s_i = lax.broadcasted_iota(jnp.int32, (L, L), 0)
    t_i = lax.broadcasted_iota(jnp.int32, (L, L), 1)
    causal = t_i >= s_i
    for q in range(S5_GROUP):
        taps = kern_ref[0, q]
        for p in range(S5_GROUP):
            blk = pltpu.roll(jnp.broadcast_to(taps[p:p + 1, :], (L, L)), 0, 1, stride=1, stride_axis=0)
            toep_sc[q * L:(q + 1) * L, p * L:(p + 1) * L] = jnp.where(causal, blk, 0.0).astype(BF16)
    u = u_ref[0]
    R = u.shape[0]
    y = _dot(u, toep_sc[...])
    x = _dot(u, we_ref[0])
    cidx = lax.broadcasted_iota(jnp.int32, (R, LANES), 0) % nchunk
    mc = mc_ref[0]
    ms = ms_ref[0]
    for kk in range(nsteps):
        sh = 1 << kk
        xs = jnp.where(cidx >= sh, pltpu.roll(x, sh, 0), 0.0)
        x = x + xs * mc[kk:kk + 1, :] + pltpu.roll(xs, 64, 1) * ms[kk:kk + 1, :]
    xprev = jnp.where(cidx >= 1, pltpu.roll(x, 1, 0), 0.0)
    y_ref[0] = y + _dot(xprev.astype(BF16), wc_ref[0])


def _s5_prep(a_re, a_im, log_step, b_re, b_im, c_re, c_im, L, nsteps):
    G, N, P = S5_GROUPS, S5_STATE, S5_GROUP
    dt = jnp.exp(log_step)[:, None]
    lr, li = a_re, a_im
    la, th = lr * dt, li * dt
    mag = jnp.exp(la)
    ar, ai = mag * jnp.cos(th), mag * jnp.sin(th)
    den = lr * lr + li * li
    cr = ((ar - 1.0) * lr + ai * li) / den
    ci = (ai * lr - (ar - 1.0) * li) / den
    bbr = cr[..., None] * b_re - ci[..., None] * b_im
    bbi = cr[..., None] * b_im + ci[..., None] * b_re
    tau = jnp.arange(L + 1, dtype=F32)[:, None, None]
    pm = jnp.exp(la[None] * tau)
    pr, pi = pm * jnp.cos(th[None] * tau), pm * jnp.sin(th[None] * tau)
    car = c_re[None] * pr[:, :, None, :] - c_im[None] * pi[:, :, None, :]
    cai = c_re[None] * pi[:, :, None, :] + c_im[None] * pr[:, :, None, :]
    kern = (jnp.einsum('tgpn,gnq->gqpt', car[:L], bbr, precision=HI)
            - jnp.einsum('tgpn,gnq->gqpt', cai[:L], bbi, precision=HI))
    rev_r, rev_i = pr[L - 1::-1][:L], pi[L - 1::-1][:L]
    we_r = rev_r[..., None] * bbr[None] - rev_i[..., None] * bbi[None]
    we_i = rev_r[..., None] * bbi[None] + rev_i[..., None] * bbr[None]
    we = jnp.concatenate([we_r, we_i], axis=2).transpose(1, 3, 0, 2).reshape(G, P * L, 2 * N)
    wc = jnp.concatenate([car[1:], -cai[1:]], axis=3)
    wc = wc.transpose(1, 3, 2, 0).reshape(G, 2 * N, P * L)
    zr, zi = pr[L], pi[L]
    mcs, mss = [], []
    for _ in range(nsteps):
        mcs.append(jnp.concatenate([zr, zr], axis=-1))
        mss.append(jnp.concatenate([-zi, zi], axis=-1))
        zr, zi = zr * zr - zi * zi, 2.0 * zr * zi
    mc = jnp.stack(mcs, axis=1)
    ms = jnp.stack(mss, axis=1)
    return kern, we.astype(BF16), wc.astype(BF16), mc, ms


def _s5_scan_steps(S):
    return max(1, (S // S5_CHUNK - 1).bit_length())


def _s5_scan(u, prep):
    B, S, W = u.shape
    G, P, N, L = S5_GROUPS, S5_GROUP, S5_STATE, S5_CHUNK
    nchunk = S // L
    nsteps = _s5_scan_steps(S)
    kern, we, wc, mc, ms = prep
    R = B * nchunk
    ug = u.astype(BF16).reshape(B, nchunk, L, G, P).transpose(3, 0, 1, 4, 2).reshape(G, R, P * L)
    grp = lambda a: pl.BlockSpec((1,) + a.shape[1:], lambda g: (g,) + (0,) * (a.ndim - 1))
    y = pl.pallas_call(
        functools.partial(_s5_kernel, nchunk=nchunk, nsteps=nsteps),
        out_shape=jax.ShapeDtypeStruct((G, R, P * L), F32),
        grid=(G,),
        in_specs=[grp(ug), grp(kern), grp(we), grp(wc), grp(mc), grp(ms)],
        out_specs=pl.BlockSpec((1, R, P * L), lambda g: (g, 0, 0)),
        scratch_shapes=[pltpu.VMEM((P * L, P * L), BF16)],
        compiler_params=_cparams(("arbitrary",)),
        name="s5_conv",
    )(ug, kern, we, wc, mc, ms)
    return y.reshape(G, B, nchunk, P, L).transpose(1, 2, 4, 0, 3).reshape(B, S, W)


def _pick_tile(S, pref):
    t = min(pref, S)
    assert S % t == 0, (S, t)
    return t


def kernel(x, c, ada_w, ada_b, ln_g, ln_b, att_w_in, mla_q_norm, mla_w_uq, mla_kv_norm, mla_w_ukv, swa_sinks, att_w_out, rec_w_in, s5_a_re, s5_a_im, s5_log_step, s5_b_re, s5_b_im, s5_c_re, s5_c_im, s5_d, s5_glu_w, s5_glu_b, rec_w_out, mlp_w1, mlp_w2):
    B, S, D = x.shape
    assert D == D_MODEL and S % SWA_WINDOW == 0 and S % S5_CHUNK == 0
    tm = _pick_tile(S, 512)

    mods = _modulation(c, ada_w, ada_b)
    zeros = jnp.zeros((B, D), F32)

    def in_mod(m):
        rows = [m[:, :D], m[:, D:2 * D]] + [zeros] * 6
        return jnp.stack(rows, axis=1)

    def post_mod(l, m1, m2):
        bc = lambda v: jnp.broadcast_to(v[None, :], (B, D))
        rows = [m1[:, 2 * D:], m2[:, :D], m2[:, D:2 * D], m2[:, 2 * D:],
                bc(ln_g[l, 0]), bc(ln_b[l, 0]), bc(ln_g[l, 1]), bc(ln_b[l, 1])]
        return jnp.stack(rows, axis=1)

    src_in, src_uq, src_uk, src_uv = _att_layouts()
    att_tab = jnp.concatenate(_rope_pieces(S, MLA_ROPE) + _rope_pieces(S, SWA_HD), axis=1)
    rec_tab = jnp.concatenate(_rope_pieces(S, RET_DK), axis=1)

    w_in_att = _place_cols(att_w_in, src_in).astype(BF16)
    w_uq = _place_cols(mla_w_uq, src_uq).astype(BF16)
    w_uk = _place_cols(mla_w_ukv, src_uk).astype(BF16)
    w_uv = _place_cols(mla_w_ukv, src_uv).astype(BF16)
    q_norm = mla_q_norm[:, None, :]
    kv_norm = mla_kv_norm[:, None, :]
    w_out_att = att_w_out.astype(BF16)
    w_in_rec = rec_w_in.astype(BF16)
    w_out_rec = rec_w_out.astype(BF16)
    glu_w = s5_glu_w.astype(BF16)
    s5_dd = s5_d[:, None, :]
    glu_b = s5_glu_b[:, None, :]
    w1 = mlp_w1.astype(BF16)
    w2 = mlp_w2.astype(BF16)
    s5_prep = jax.vmap(functools.partial(_s5_prep, L=S5_CHUNK, nsteps=_s5_scan_steps(S)))(
        s5_a_re, s5_a_im, s5_log_step, s5_b_re, s5_b_im, s5_c_re, s5_c_im)

    for l in range(DEPTH):
        j = l // 2
        m1, m2 = mods[2 * l], mods[2 * l + 1]
        pmod = post_mod(l, m1, m2)
        if l % 2 == 0:
            qm, km, vt, qs, ks, vs = _att_in(x, in_mod(m1), j, w_in_att, q_norm, w_uq, kv_norm, w_uk, w_uv,
                                             att_tab, tm)
            o_a = _mla_attention(qm, km, vt, _pick_tile(S, 1024))
            o_b = _swa_attention(swa_sinks[j], qs, ks, vs, tm)
            x = _post(_att_post_kernel, x, [o_a, o_b], pmod,
                      [(w_out_att, j), (w1, l), (w2, l)], tm, "att_post")
        else:
            rq, rk, rv, rg, u = _rec_in(x, in_mod(m1), j, w_in_rec, rec_tab, tm)
            o_c = _retention(rq, rk, rv, rg, tm)
            ys = _s5_scan(u, [a[j] for a in s5_prep])
            x = _post(_rec_post_kernel, x, [o_c, ys, u], pmod,
                      [(s5_dd, j), (glu_w, j), (glu_b, j), (w_out_rec, j), (w1, l), (w2, l)], tm, "rec_post")
    return x
```

```python
import functools
import math

import jax
import jax.numpy as jnp
import numpy as np
from jax import lax
from jax.experimental import pallas as pl
from jax.experimental.pallas import tpu as pltpu

F32 = jnp.float32
BF16 = jnp.bfloat16

D_MODEL = 1024
DEPTH = 4
DN_ALPHA = (2.0 * DEPTH) ** 0.25
LN_EPS = 1e-5
RMS_EPS = 1e-6
ROPE_THETA = 10000.0
MLA_HEADS = 8
MLA_NOPE = 64
MLA_ROPE = 32
MLA_V = 64
MLA_Q_RANK = 384
MLA_KV_RANK = 256
SWA_HEADS = 8
SWA_KV_HEADS = 2
SWA_HD = 64
SWA_WINDOW = 128
RET_HEADS = 4
RET_DK = 128
RET_DV = 128
RET_CHUNK = 128
S5_WIDTH = 512
S5_GROUP = 16
S5_GROUPS = 32
S5_STATE = 64
S5_CHUNK = 128
D_FF = 4096
FF_CHUNK = 1024

LANES = 128
VMEM_LIMIT = 56 * 1024 * 1024
NEG_BIG = -1e30
HI = lax.Precision.HIGHEST
MLA_VX = MLA_V + 16
MLA_Q_SCALE = (MLA_NOPE + MLA_ROPE) ** -0.5 * math.log2(math.e)


def _cparams(sem):
    return pltpu.CompilerParams(dimension_semantics=sem, vmem_limit_bytes=VMEM_LIMIT)


def _const_spec(shape):
    nd = len(shape)
    return pl.BlockSpec(shape, lambda *_: (0,) * nd, pipeline_mode=pl.Buffered(1))


def _layer_spec(arr, lyr):
    return pl.BlockSpec((None,) + arr.shape[1:], lambda *_: (lyr, 0, 0), pipeline_mode=pl.Buffered(1))


def _dot(a, b):
    return jnp.dot(a, b, preferred_element_type=F32)


def _dot_nt(a, b):
    return lax.dot_general(a, b, (((1,), (1,)), ((), ())), preferred_element_type=F32)


def _dot_tn(a, b):
    return lax.dot_general(a, b, (((0,), (0,)), ((), ())), preferred_element_type=F32)


def _rope(x, c, s):
    return x * c + pltpu.roll(x, 64, 1) * s


def _layer_norm(v, g, b):
    mu = jnp.mean(v, axis=-1, keepdims=True)
    d = v - mu
    var = jnp.mean(d * d, axis=-1, keepdims=True)
    return d * lax.rsqrt(var + LN_EPS) * g + b


def _mod_kernel(c_ref, w_ref, b_ref, o_ref):
    cond = jax.nn.silu(c_ref[...]).astype(BF16)
    o_ref[0] = _dot(cond, w_ref[0].astype(BF16)) + b_ref[0]


def _modulation(c, ada_w, ada_b):
    B = c.shape[0]
    nmod = ada_w.shape[0] * ada_w.shape[1]
    rows = 8
    tn = 768
    c8 = jnp.zeros((rows, D_MODEL), F32).at[:B].set(c)
    w = ada_w.reshape(nmod, D_MODEL, 3 * D_MODEL)
    b = ada_b.reshape(nmod, 1, 3 * D_MODEL)
    out = pl.pallas_call(
        _mod_kernel,
        out_shape=jax.ShapeDtypeStruct((nmod, rows, 3 * D_MODEL), F32),
        grid=(nmod, 3 * D_MODEL // tn),
        in_specs=[
            pl.BlockSpec((rows, D_MODEL), lambda m, n: (0, 0)),
            pl.BlockSpec((1, D_MODEL, tn), lambda m, n: (m, 0, n)),
            pl.BlockSpec((1, 1, tn), lambda m, n: (m, 0, n)),
        ],
        out_specs=pl.BlockSpec((1, rows, tn), lambda m, n: (m, 0, n)),
        compiler_params=_cparams(("arbitrary", "arbitrary")),
        name="adaln_mod",
    )(c8, w, b)
    return out[:, :B, :]


def _rope_pieces(seq, dim):
    half = dim // 2
    inv = ROPE_THETA ** (-jnp.arange(0, dim, 2, dtype=F32) / dim)
    ang = jnp.arange(seq, dtype=F32)[:, None] * inv[None, :]
    cos = jnp.tile(jnp.cos(ang), (1, LANES // half))
    sin = jnp.tile(jnp.sin(ang), (1, LANES // half))
    lane = np.arange(LANES)
    lo = jnp.asarray(lane < half)
    hi = jnp.asarray((lane >= 64) & (lane < 64 + half))
    return [jnp.where(lo | hi, cos, 1.0), jnp.where(lo, -sin, jnp.where(hi, sin, 0.0))]


def _place_cols(w, src):
    src = np.asarray(src)
    pieces, start = [], 0
    while start < len(src):
        end = start + 1
        while end < len(src) and ((src[start] < 0 and src[end] < 0)
                                  or (src[start] >= 0 and src[end] == src[end - 1] + 1)):
            end += 1
        if src[start] < 0:
            pieces.append(jnp.zeros(w.shape[:-1] + (end - start,), w.dtype))
        else:
            pieces.append(w[..., src[start]:src[start] + end - start])
        start = end
    return jnp.concatenate(pieces, axis=-1)


def _att_layouts():
    off_kr = MLA_Q_RANK + MLA_KV_RANK
    off_sq = off_kr + MLA_ROPE
    off_sk = off_sq + SWA_HEADS * SWA_HD
    off_sv = off_sk + SWA_KV_HEADS * SWA_HD
    src = list(range(off_kr))
    grp = [-1] * LANES
    for i in range(16):
        grp[i] = off_kr + i
        grp[64 + i] = off_kr + 16 + i
    src += grp
    for base, nh in ((off_sq, SWA_HEADS), (off_sk, SWA_KV_HEADS)):
        for h in range(nh):
            grp = [-1] * LANES
            for i in range(32):
                grp[i] = base + h * SWA_HD + i
                grp[64 + i] = base + h * SWA_HD + 32 + i
            src += grp
    src += list(range(off_sv, off_sv + SWA_KV_HEADS * SWA_HD))
    uq, uk, uv = [], [], []
    for h in range(MLA_HEADS):
        qb = h * (MLA_NOPE + MLA_ROPE)
        kb = h * (MLA_NOPE + MLA_V)
        gq = [-1] * LANES
        gk = [-1] * LANES
        for i in range(16):
            gq[i] = qb + MLA_NOPE + i
            gq[64 + i] = qb + MLA_NOPE + 16 + i
        for i in range(48):
            gq[16 + i] = qb + i
            gk[16 + i] = kb + i
        for i in range(16):
            gq[80 + i] = qb + 48 + i
            gk[80 + i] = kb + 48 + i
        uq += gq
        uk += gk
        uv += list(range(kb + MLA_NOPE, kb + MLA_NOPE + MLA_V))
    return src, uq, uk, uv


def _att_in_kernel(x_ref, mod_ref, win_ref, qn_ref, wuq_ref, kvn_ref, wuk_ref, wuv_ref, tab_ref,
                   qm_ref, km_ref, vm_ref, qs_ref, ks_ref, vs_ref):
    mod = mod_ref[0]
    h = (x_ref[0] * (1.0 + mod[1:2, :]) + mod[0:1, :]).astype(BF16)
    z = _dot(h, win_ref[...])
    tab = tab_ref[...]
    t = lambda k: tab[:, k * LANES:(k + 1) * LANES]

    c_q = z[:, :MLA_Q_RANK]
    c_q = c_q * lax.rsqrt(jnp.mean(c_q * c_q, axis=-1, keepdims=True) + RMS_EPS) * qn_ref[...]
    q = _dot(c_q.astype(BF16), wuq_ref[...])
    c_kv = z[:, MLA_Q_RANK:MLA_Q_RANK + MLA_KV_RANK]
    c_kv = c_kv * lax.rsqrt(jnp.mean(c_kv * c_kv, axis=-1, keepdims=True) + RMS_EPS) * kvn_ref[...]
    c_kv = c_kv.astype(BF16)
    kup = _dot(c_kv, wuk_ref[...])
    v = _dot(c_kv, wuv_ref[...])
    off = MLA_Q_RANK + MLA_KV_RANK
    kr = _rope(z[:, off:off + LANES], t(0), t(1))
    for hd in range(MLA_HEADS):
        sl = slice(hd * LANES, (hd + 1) * LANES)
        qm_ref[0, hd] = (_rope(q[:, sl], t(0), t(1)) * MLA_Q_SCALE).astype(BF16)
        km_ref[0, hd] = (kup[:, sl] + kr).astype(BF16)
    tm = v.shape[0]
    ones_row = jnp.where(lax.broadcasted_iota(jnp.int32, (MLA_VX - MLA_V, tm), 0) == 0, 1.0, 0.0)
    for hp in range(MLA_HEADS // 2):
        vt = v[:, hp * LANES:(hp + 1) * LANES].T
        vm_ref[0, hp, 0] = jnp.concatenate(
            [vt[:MLA_V], ones_row, vt[MLA_V:], ones_row], axis=0).astype(BF16)
    off += LANES
    for hd in range(SWA_HEADS):
        qs_ref[0, hd] = (_rope(z[:, off + hd * LANES:off + (hd + 1) * LANES], t(2), t(3))
                         * SWA_HD ** -0.5).astype(BF16)
    off += SWA_HEADS * LANES
    for kv in range(SWA_KV_HEADS):
        ks_ref[0, kv] = _rope(z[:, off + kv * LANES:off + (kv + 1) * LANES], t(2), t(3)).astype(BF16)
    off += SWA_KV_HEADS * LANES
    sv = z[:, off:off + LANES]
    vs_ref[0, 0] = sv.astype(BF16)
    vs_ref[0, 1] = pltpu.roll(sv, 64, 1).astype(BF16)


def _att_in(x, mod, lyr, w_in_p, q_norm, w_uq_p, kv_norm, w_uk_p, w_uv_p, tab, tm):
    B, S, D = x.shape
    bf = lambda *s: jax.ShapeDtypeStruct(s, BF16)
    tok4 = lambda nh: pl.BlockSpec((1, nh, tm, LANES), lambda b, i: (b, 0, i, 0))
    return pl.pallas_call(
        _att_in_kernel,
        out_shape=(bf(B, MLA_HEADS, S, LANES), bf(B, MLA_HEADS, S, LANES),
                   bf(B, MLA_HEADS // 2, S // tm, 2 * MLA_VX, tm),
                   bf(B, SWA_HEADS, S, LANES), bf(B, SWA_KV_HEADS, S, LANES), bf(B, 2, S, LANES)),
        grid=(B, S // tm),
        in_specs=[
            pl.BlockSpec((1, tm, D), lambda b, i: (b, i, 0)),
            pl.BlockSpec((1, 8, D), lambda b, i: (b, 0, 0)),
            _layer_spec(w_in_p, lyr), _layer_spec(q_norm, lyr), _layer_spec(w_uq_p, lyr),
            _layer_spec(kv_norm, lyr), _layer_spec(w_uk_p, lyr), _layer_spec(w_uv_p, lyr),
            pl.BlockSpec((tm, tab.shape[1]), lambda b, i: (i, 0)),
        ],
        out_specs=(tok4(MLA_HEADS), tok4(MLA_HEADS),
                   pl.BlockSpec((1, MLA_HEADS // 2, 1, 2 * MLA_VX, tm), lambda b, i: (b, 0, i, 0, 0)),
                   tok4(SWA_HEADS), tok4(SWA_KV_HEADS), tok4(2)),
        compiler_params=_cparams(("arbitrary", "arbitrary")),
        name="att_in_proj",
    )(x, mod, w_in_p, q_norm, w_uq_p, kv_norm, w_uk_p, w_uv_p, tab)


def _mla_kernel(q_ref, k_ref, vt_ref, o_ref, st0a, st0b, st1a, st1b, p0, p1, acc_sc, cm_sc, *, t, h):
    i = pl.program_id(2)
    assert t == 4 * h
    st_sets = ((st0a, st0b), (st1a, st1b))
    p_sets = (p0, p1)

    def scores(blk, st_set, half):
        k0 = pl.multiple_of(blk * h, h)
        for hh in range(2):
            st = _dot_nt(k_ref[0, hh, pl.ds(k0, h), :], q_ref[0, hh])
            st_sets[st_set][half][hh] = st
            cm_sc[(st_set * 2 + half) * 2 + hh, 0:1, :] = jnp.max(st, axis=0, keepdims=True)

    def values(vt_blk, p_rd, hh):
        vt = vt_ref[0, 0, vt_blk, hh * MLA_VX:(hh + 1) * MLA_VX, :]
        return _dot(vt, p_rd[hh])

    def accumulate(vt_blk, carry, p_rd):
        for hh in range(2):
            acc_sc[hh] = carry[hh][1] * acc_sc[hh] + values(vt_blk, p_rd, hh)

    def softmax(carry, par, diag):
        out = []
        for hh in range(2):
            m = carry[hh][0]
            sts, m_new = [], m
            for half in range(2):
                st = st_sets[par][half][hh]
                if diag is None:
                    blk_max = cm_sc[(par * 2 + half) * 2 + hh, 0:1, :]
                else:
                    key = lax.broadcasted_iota(jnp.int32, (h, t), 0) + (2 * diag + half) * h
                    qry = lax.broadcasted_iota(jnp.int32, (h, t), 1)
                    st = jnp.where(key <= qry, st, NEG_BIG)
                    blk_max = jnp.max(st, axis=0, keepdims=True)
                sts.append(st)
                m_new = jnp.maximum(m_new, blk_max)
            for half in range(2):
                p_sets[par][hh, half * h:(half + 1) * h, :] = jnp.exp2(sts[half] - m_new).astype(BF16)
            out.append((m_new, jnp.exp2(m - m_new)))
        return tuple(out)

    def iteration(c, par, carry, diag=None, more=True):
        if more:
            scores(2 * c + 2, 1 - par, 0)
            scores(2 * c + 3, 1 - par, 1)
        accumulate(jnp.maximum(c - 1, 0), carry, p_sets[1 - par])
        return softmax(carry, par, diag)

    def two_pairs(cc, carry):
        carry = iteration(2 * cc, 0, carry)
        return iteration(2 * cc + 1, 1, carry)

    scores(0, 0, 0)
    scores(1, 0, 1)
    p1[...] = jnp.zeros(p1.shape, BF16)
    acc_sc[...] = jnp.zeros(acc_sc.shape, F32)
    init = (jnp.full((1, t), NEG_BIG, F32), jnp.ones((1, t), F32))
    carry = lax.fori_loop(0, i, two_pairs, (init, init))
    carry = iteration(2 * i, 0, carry, diag=0)
    carry = iteration(2 * i + 1, 1, carry, diag=1, more=False)
    accumulate(2 * i + 1, carry, p_sets[1])
    outs = []
    for hh in range(2):
        acc = acc_sc[hh]
        outs.append(acc[:MLA_V] / acc[MLA_V:MLA_V + 1])
    o_ref[0] = jnp.concatenate(outs, axis=0).T.astype(BF16)


def _mla_attention(qm, km, vt, t):
    B, H, S, _ = qm.shape
    tv = vt.shape[-1]
    h = tv // 2
    assert t % tv == 0
    return pl.pallas_call(
        functools.partial(_mla_kernel, t=t, h=h),
        out_shape=jax.ShapeDtypeStruct((B, S, H * MLA_V), BF16),
        grid=(B, H // 2, S // t),
        in_specs=[
            pl.BlockSpec((1, 2, t, LANES), lambda b, hp, i: (b, hp, i, 0)),
            pl.BlockSpec((1, 2, S, LANES), lambda b, hp, i: (b, hp, 0, 0)),
            pl.BlockSpec((1, 1, S // tv, 2 * MLA_VX, tv), lambda b, hp, i: (b, hp, 0, 0, 0)),
        ],
        out_specs=pl.BlockSpec((1, t, LANES), lambda b, hp, i: (b, i, hp)),
        scratch_shapes=([pltpu.VMEM((2, h, t), F32)] * 4 + [pltpu.VMEM((2, 2 * h, t), BF16)] * 2
                        + [pltpu.VMEM((2, MLA_VX, t), F32), pltpu.VMEM((8, 8, t), F32)]),
        compiler_params=_cparams(("arbitrary", "arbitrary", "arbitrary")),
        name="mla_flash",
    )(qm, km, vt)


def _swa_kernel(sink_ref, q_ref, kc_ref, kp_ref, vc_ref, vp_ref, o_ref, s_sc, p_sc, *, tq):
    W = SWA_WINDOW
    G = SWA_HEADS // SWA_KV_HEADS
    i = pl.program_id(1)
    nsub = tq // W
    r = lax.broadcasted_iota(jnp.int32, (G * W, 2 * W), 0) % W
    c = lax.broadcasted_iota(jnp.int32, (G * W, 2 * W), 1)
    band = (c > r) & (c <= r + W)
    first = band & ((i > 0) | (c >= W))
    lane = lax.broadcasted_iota(jnp.int32, (W, LANES), 1)
    for kv in range(SWA_KV_HEADS):
        kcat = jnp.concatenate([kp_ref[0, kv], kc_ref[0, kv]], axis=0)
        for n in range(nsub):
            q4 = jnp.concatenate([q_ref[0, kv * G + g, n * W:(n + 1) * W, :] for g in range(G)], axis=0)
            s_sc[kv * nsub + n] = _dot_nt(q4, kcat[n * W:(n + 2) * W])
    for kv in range(SWA_KV_HEADS):
        sink = jnp.concatenate(
            [jnp.full((W, 1), sink_ref[kv * G + g], F32) for g in range(G)], axis=0)
        for n in range(nsub):
            s = jnp.where(first if n == 0 else band, s_sc[kv * nsub + n], NEG_BIG)
            m = jnp.maximum(jnp.max(s, axis=-1, keepdims=True), sink)
            p = jnp.exp(s - m)
            den = jnp.sum(p, axis=-1, keepdims=True) + jnp.exp(sink - m)
            p_sc[kv * nsub + n] = (p / den).astype(BF16)
    va = jnp.concatenate([vp_ref[0, 0], vc_ref[0, 0]], axis=0)
    vb = jnp.concatenate([vp_ref[0, 1], vc_ref[0, 1]], axis=0)
    for kv in range(SWA_KV_HEADS):
        v_even, v_odd = (va, vb) if kv == 0 else (vb, va)
        for n in range(nsub):
            for pr in range(G // 2):
                oe = _dot(p_sc[kv * nsub + n, (2 * pr) * W:(2 * pr + 1) * W, :], v_even[n * W:(n + 2) * W])
                oo = _dot(p_sc[kv * nsub + n, (2 * pr + 1) * W:(2 * pr + 2) * W, :], v_odd[n * W:(n + 2) * W])
                blk = kv * (G // 2) + pr
                o_ref[0, n * W:(n + 1) * W, blk * LANES:(blk + 1) * LANES] = (
                    jnp.where(lane < SWA_HD, oe, oo).astype(BF16))


def _swa_attention(sinks, qs, ks, vs, tq):
    B, _, S, _ = qs.shape
    W = SWA_WINDOW
    nb = tq // W
    cur = lambda nh: pl.BlockSpec((1, nh, tq, LANES), lambda b, i: (b, 0, i, 0))
    prev = lambda nh: pl.BlockSpec((1, nh, W, LANES), lambda b, i: (b, 0, jnp.maximum(i * nb - 1, 0), 0))
    return pl.pallas_call(
        functools.partial(_swa_kernel, tq=tq),
        out_shape=jax.ShapeDtypeStruct((B, S, SWA_HEADS * SWA_HD), BF16),
        grid=(B, S // tq),
        in_specs=[
            pl.BlockSpec(memory_space=pltpu.SMEM),
            cur(SWA_HEADS), cur(SWA_KV_HEADS), prev(SWA_KV_HEADS), cur(2), prev(2),
        ],
        out_specs=pl.BlockSpec((1, tq, SWA_HEADS * SWA_HD), lambda b, i: (b, i, 0)),
        scratch_shapes=[pltpu.VMEM((SWA_KV_HEADS * nb, SWA_HEADS // SWA_KV_HEADS * W, 2 * W), F32),
                        pltpu.VMEM((SWA_KV_HEADS * nb, SWA_HEADS // SWA_KV_HEADS * W, 2 * W), BF16)],
        compiler_params=_cparams(("arbitrary", "arbitrary")),
        name="swa_attn",
    )(sinks, qs, ks, ks, vs, vs)


def _mlp_tail(x, y, mod, w1_ref, w2_ref):
    x1 = _layer_norm(DN_ALPHA * x + (1.0 + mod[0:1, :]) * y, mod[4:5, :], mod[5:6, :])
    h = (x1 * (1.0 + mod[2:3, :]) + mod[1:2, :]).astype(BF16)
    acc = None
    for cidx in range(D_FF // FF_CHUNK):
        hid = _dot(h, w1_ref[:, cidx * FF_CHUNK:(cidx + 1) * FF_CHUNK])
        hid = jnp.square(jnp.maximum(hid, 0.0)).astype(BF16)
        part = _dot(hid, w2_ref[cidx * FF_CHUNK:(cidx + 1) * FF_CHUNK, :])
        acc = part if acc is None else acc + part
    return _layer_norm(DN_ALPHA * x1 + (1.0 + mod[3:4, :]) * acc, mod[6:7, :], mod[7:8, :])


def _att_post_kernel(x_ref, oa_ref, ob_ref, mod_ref, wout_ref, w1_ref, w2_ref, o_ref):
    half = wout_ref.shape[0] // 2
    y = _dot(oa_ref[0], wout_ref[:half, :]) + _dot(ob_ref[0], wout_ref[half:, :])
    o_ref[0] = _mlp_tail(x_ref[0], y, mod_ref[0], w1_ref, w2_ref)


def _rec_post_kernel(x_ref, oc_ref, ys_ref, u_ref, mod_ref, d_ref, gw_ref, gb_ref,
                     wout_ref, w1_ref, w2_ref, o_ref):
    half = wout_ref.shape[0] // 2
    yy = jax.nn.gelu(ys_ref[0] + d_ref[...] * u_ref[0])
    od = yy * jax.nn.sigmoid(_dot(yy.astype(BF16), gw_ref[...]) + gb_ref[...])
    y = _dot(oc_ref[0], wout_ref[:half, :]) + _dot(od.astype(BF16), wout_ref[half:, :])
    o_ref[0] = _mlp_tail(x_ref[0], y, mod_ref[0], w1_ref, w2_ref)


def _post(kernel, x, toks, mod, params, tm, name):
    B, S, D = x.shape
    tok = lambda a: pl.BlockSpec((1, tm, a.shape[-1]), lambda b, i: (b, i, 0))
    return pl.pallas_call(
        kernel,
        out_shape=jax.ShapeDtypeStruct((B, S, D), F32),
        grid=(B, S // tm),
        in_specs=([tok(x)] + [tok(a) for a in toks]
                  + [pl.BlockSpec((1, 8, D), lambda b, i: (b, 0, 0))]
                  + [_layer_spec(a, lyr) for a, lyr in params]),
        out_specs=tok(x),
        compiler_params=_cparams(("arbitrary", "arbitrary")),
        name=name,
    )(x, *toks, mod, *[a for a, _ in params])


def _rec_in_kernel(x_ref, mod_ref, win_ref, tab_ref, q_ref, k_ref, v_ref, g_ref, u_ref):
    mod = mod_ref[0]
    h = (x_ref[0] * (1.0 + mod[1:2, :]) + mod[0:1, :]).astype(BF16)
    z = _dot(h, win_ref[...])
    tab = tab_ref[...]
    t = lambda k: tab[:, k * LANES:(k + 1) * LANES]
    W = RET_HEADS * RET_DK
    for hd in range(RET_HEADS):
        sl = slice(hd * LANES, (hd + 1) * LANES)
        q_ref[0, :, sl] = _rope(z[:, hd * LANES:(hd + 1) * LANES], t(0), t(1)).astype(BF16)
        k_ref[0, :, sl] = _rope(z[:, W + hd * LANES:W + (hd + 1) * LANES], t(0), t(1)) * RET_DK ** -0.5
    v_ref[0] = z[:, 2 * W:3 * W].astype(BF16)
    g_ref[0] = z[:, 3 * W:4 * W]
    u_ref[0] = z[:, 4 * W:]


def _rec_in(x, mod, lyr, w_in, tab, tm):
    B, S, D = x.shape
    W = RET_HEADS * RET_DK
    sd = lambda dt: jax.ShapeDtypeStruct((B, S, W), dt)
    tok = pl.BlockSpec((1, tm, W), lambda b, i: (b, i, 0))
    return pl.pallas_call(
        _rec_in_kernel,
        out_shape=(sd(BF16), sd(F32), sd(BF16), sd(F32), sd(F32)),
        grid=(B, S // tm),
        in_specs=[
            pl.BlockSpec((1, tm, D), lambda b, i: (b, i, 0)),
            pl.BlockSpec((1, 8, D), lambda b, i: (b, 0, 0)),
            _layer_spec(w_in, lyr),
            pl.BlockSpec((tm, tab.shape[1]), lambda b, i: (i, 0)),
        ],
        out_specs=(tok, tok, tok, tok, tok),
        compiler_params=_cparams(("arbitrary", "arbitrary")),
        name="rec_in_proj",
    )(x, mod, w_in, tab)


def _ret_kernel(q_ref, k_ref, v_ref, g_ref, dec_ref, te_ref, fs_ref, cd_ref, o_ref,
                st_sc, sc_sc, upd_sc, sin_sc, *, tm):
    C = RET_CHUNK
    nch = tm // C

    @pl.when(pl.program_id(1) == 0)
    def _():
        st_sc[...] = jnp.zeros(st_sc.shape, F32)

    def operands(hd, n):
        sl = slice(hd * LANES, (hd + 1) * LANES)
        rows = slice(n * C, (n + 1) * C)
        return rows, sl

    for hd in range(RET_HEADS):
        for n in range(nch):
            rows, sl = operands(hd, n)
            k = k_ref[0, rows, sl]
            v = v_ref[0, rows, sl]
            sc_sc[hd * nch + n] = (_dot_nt(q_ref[0, rows, sl], k.astype(BF16)) * dec_ref[hd]).astype(BF16)
            upd_sc[hd * nch + n] = _dot_tn((k * te_ref[hd]).astype(BF16), v)
    for hd in range(RET_HEADS):
        state = st_sc[hd]
        for n in range(nch):
            sin_sc[hd * nch + n] = state.astype(BF16)
            state = cd_ref[hd] * state + upd_sc[hd * nch + n]
        st_sc[hd] = state
    for hd in range(RET_HEADS):
        for n in range(nch):
            rows, sl = operands(hd, n)
            o = (_dot(sc_sc[hd * nch + n], v_ref[0, rows, sl])
                 + _dot(q_ref[0, rows, sl], sin_sc[hd * nch + n]) * fs_ref[hd])
            mu = jnp.mean(o, axis=-1, keepdims=True)
            d = o - mu
            var = jnp.mean(d * d, axis=-1, keepdims=True)
            o = d * lax.rsqrt(var + LN_EPS)
            o_ref[0, rows, sl] = (jax.nn.silu(g_ref[0, rows, sl]) * o).astype(BF16)


def _retention(q, k, v, g, tm):
    B, S, W = q.shape
    H, C = RET_HEADS, RET_CHUNK
    log_gamma = jnp.log(1.0 - 2.0 ** (-5.0 - jnp.arange(H, dtype=F32)))
    idx = jnp.arange(C, dtype=F32)
    diff = idx[:, None] - idx[None, :]
    decay = jnp.where(diff >= 0, jnp.exp(log_gamma[:, None, None] * jnp.maximum(diff, 0.0)), 0.0)
    to_end = jnp.exp(log_gamma[:, None] * (C - 1.0 - idx)[None, :])
    from_start = jnp.exp((idx + 1.0)[None, :] * log_gamma[:, None])
    chunk_decay = jnp.exp(log_gamma * C)
    bc = lambda a: jnp.broadcast_to(a[:, :, None], (H, C, LANES))
    cd = jnp.broadcast_to(chunk_decay[:, None, None], (H, RET_DK, RET_DV))
    tok = pl.BlockSpec((1, tm, W), lambda b, i: (b, i, 0))
    tbl = _const_spec((H, C, LANES))
    return pl.pallas_call(
        functools.partial(_ret_kernel, tm=tm),
        out_shape=jax.ShapeDtypeStruct((B, S, W), BF16),
        grid=(B, S // tm),
        in_specs=[tok, tok, tok, tok, tbl, tbl, tbl, tbl],
        out_specs=tok,
        scratch_shapes=[pltpu.VMEM((H, RET_DK, RET_DV), F32),
                        pltpu.VMEM((H * tm // C, C, C), BF16),
                        pltpu.VMEM((H * tm // C, RET_DK, RET_DV), F32),
                        pltpu.VMEM((H * tm // C, RET_DK, RET_DV), BF16)],
        compiler_params=_cparams(("arbitrary", "arbitrary")),
        name="retention",
    )(q, k, v, g, decay, bc(to_end), bc(from_start), cd)


def _s5_kernel(u_ref, kern_ref, we_ref, wc_ref, mc_ref, ms_ref, y_ref, toep_sc, *, nchunk, nsteps):
    L = S5_CHUNK
    s_i = lax.broadcasted_iota(jnp.int32, (L, L), 0)
    t_i = lax.broadcasted_iota(jnp.int32, (L, L), 1)
    causal = t_i >= s_i
    for q in range(S5_GROUP):
        taps = kern_ref[0, q]
        for p in range(S5_GROUP):
            blk = pltpu.roll(jnp.broadcast_to(taps[p:p + 1, :], (L, L)), 0, 1, stride=1, stride_axis=0)
            toep_sc[q * L:(q + 1) * L, p * L:(p + 1) * L] = jnp.where(causal, blk, 0.0).astype(BF16)
    u = u_ref[0]
    R = u.shape[0]
    y = _dot(u, toep_sc[...])
    x = _dot(u, we_ref[0])
    cidx = lax.broadcasted_iota(jnp.int32, (R, LANES), 0) % nchunk
    mc = mc_ref[0]
    ms = ms_ref[0]
    for kk in range(nsteps):
        sh = 1 << kk
        xs = jnp.where(cidx >= sh, pltpu.roll(x, sh, 0), 0.0)
        x = x + xs * mc[kk:kk + 1, :] + pltpu.roll(xs, 64, 1) * ms[kk:kk + 1, :]
    xprev = jnp.where(cidx >= 1, pltpu.roll(x, 1, 0), 0.0)
    y_ref[0] = y + _dot(xprev.astype(BF16), wc_ref[0])


def _s5_prep(a_re, a_im, log_step, b_re, b_im, c_re, c_im, L, nsteps):
    G, N, P = S5_GROUPS, S5_STATE, S5_GROUP
    dt = jnp.exp(log_step)[:, None]
    lr, li = a_re, a_im
    la, th = lr * dt, li * dt
    mag = jnp.exp(la)
    ar, ai = mag * jnp.cos(th), mag * jnp.sin(th)
    den = lr * lr + li * li
    cr = ((ar - 1.0) * lr + ai * li) / den
    ci = (ai * lr - (ar - 1.0) * li) / den
    bbr = cr[..., None] * b_re - ci[..., None] * b_im
    bbi = cr[..., None] * b_im + ci[..., None] * b_re
    tau = jnp.arange(L + 1, dtype=F32)[:, None, None]
    pm = jnp.exp(la[None] * tau)
    pr, pi = pm * jnp.cos(th[None] * tau), pm * jnp.sin(th[None] * tau)
    car = c_re[None] * pr[:, :, None, :] - c_im[None] * pi[:, :, None, :]
    cai = c_re[None] * pi[:, :, None, :] + c_im[None] * pr[:, :, None, :]
    kern = (jnp.einsum('tgpn,gnq->gqpt', car[:L], bbr, precision=HI)
            - jnp.einsum('tgpn,gnq->gqpt', cai[:L], bbi, precision=HI))
    rev_r, rev_i = pr[L - 1::-1][:L], pi[L - 1::-1][:L]
    we_r = rev_r[..., None] * bbr[None] - rev_i[..., None] * bbi[None]
    we_i = rev_r[..., None] * bbi[None] + rev_i[..., None] * bbr[None]
    we = jnp.concatenate([we_r, we_i], axis=2).transpose(1, 3, 0, 2).reshape(G, P * L, 2 * N)
    wc = jnp.concatenate([car[1:], -cai[1:]], axis=3)
    wc = wc.transpose(1, 3, 2, 0).reshape(G, 2 * N, P * L)
    zr, zi = pr[L], pi[L]
    mcs, mss = [], []
    for _ in range(nsteps):
        mcs.append(jnp.concatenate([zr, zr], axis=-1))
        mss.append(jnp.concatenate([-zi, zi], axis=-1))
        zr, zi = zr * zr - zi * zi, 2.0 * zr * zi
    mc = jnp.stack(mcs, axis=1)
    ms = jnp.stack(mss, axis=1)
    return kern, we.astype(BF16), wc.astype(BF16), mc, ms


def _s5_scan_steps(S):
    return max(1, (S // S5_CHUNK - 1).bit_length())


def _s5_scan(u, prep):
    B, S, W = u.shape
    G, P, N, L = S5_GROUPS, S5_GROUP, S5_STATE, S5_CHUNK
    nchunk = S // L
    nsteps = _s5_scan_steps(S)
    kern, we, wc, mc, ms = prep
    R = B * nchunk
    ug = u.astype(BF16).reshape(B, nchunk, L, G, P).transpose(3, 0, 1, 4, 2).reshape(G, R, P * L)
    grp = lambda a: pl.BlockSpec((1,) + a.shape[1:], lambda g: (g,) + (0,) * (a.ndim - 1))
    y = pl.pallas_call(
        functools.partial(_s5_kernel, nchunk=nchunk, nsteps=nsteps),
        out_shape=jax.ShapeDtypeStruct((G, R, P * L), F32),
        grid=(G,),
        in_specs=[grp(ug), grp(kern), grp(we), grp(wc), grp(mc), grp(ms)],
        out_specs=pl.BlockSpec((1, R, P * L), lambda g: (g, 0, 0)),
        scratch_shapes=[pltpu.VMEM((P * L, P * L), BF16)],
        compiler_params=_cparams(("arbitrary",)),
        name="s5_conv",
    )(ug, kern, we, wc, mc, ms)
    return y.reshape(G, B, nchunk, P, L).transpose(1, 2, 4, 0, 3).reshape(B, S, W)


def _pick_tile(S, pref):
    t = min(pref, S)
    assert S % t == 0, (S, t)
    return t


def kernel(x, c, ada_w, ada_b, ln_g, ln_b, att_w_in, mla_q_norm, mla_w_uq, mla_kv_norm, mla_w_ukv, swa_sinks, att_w_out, rec_w_in, s5_a_re, s5_a_im, s5_log_step, s5_b_re, s5_b_im, s5_c_re, s5_c_im, s5_d, s5_glu_w, s5_glu_b, rec_w_out, mlp_w1, mlp_w2):
    B, S, D = x.shape
    assert D == D_MODEL and S % SWA_WINDOW == 0 and S % S5_CHUNK == 0
    tm = _pick_tile(S, 512)

    mods = _modulation(c, ada_w, ada_b)
    zeros = jnp.zeros((B, D), F32)

    def in_mod(m):
        rows = [m[:, :D], m[:, D:2 * D]] + [zeros] * 6
        return jnp.stack(rows, axis=1)

    def post_mod(l, m1, m2):
        bc = lambda v: jnp.broadcast_to(v[None, :], (B, D))
        rows = [m1[:, 2 * D:], m2[:, :D], m2[:, D:2 * D], m2[:, 2 * D:],
                bc(ln_g[l, 0]), bc(ln_b[l, 0]), bc(ln_g[l, 1]), bc(ln_b[l, 1])]
        return jnp.stack(rows, axis=1)

    src_in, src_uq, src_uk, src_uv = _att_layouts()
    att_tab = jnp.concatenate(_rope_pieces(S, MLA_ROPE) + _rope_pieces(S, SWA_HD), axis=1)
    rec_tab = jnp.concatenate(_rope_pieces(S, RET_DK), axis=1)

    w_in_att = _place_cols(att_w_in, src_in).astype(BF16)
    w_uq = _place_cols(mla_w_uq, src_uq).astype(BF16)
    w_uk = _place_cols(mla_w_ukv, src_uk).astype(BF16)
    w_uv = _place_cols(mla_w_ukv, src_uv).astype(BF16)
    q_norm = mla_q_norm[:, None, :]
    kv_norm = mla_kv_norm[:, None, :]
    w_out_att = att_w_out.astype(BF16)
    w_in_rec = rec_w_in.astype(BF16)
    w_out_rec = rec_w_out.astype(BF16)
    glu_w = s5_glu_w.astype(BF16)
    s5_dd = s5_d[:, None, :]
    glu_b = s5_glu_b[:, None, :]
    w1 = mlp_w1.astype(BF16)
    w2 = mlp_w2.astype(BF16)
    s5_prep = jax.vmap(functools.partial(_s5_prep, L=S5_CHUNK, nsteps=_s5_scan_steps(S)))(
        s5_a_re, s5_a_im, s5_log_step, s5_b_re, s5_b_im, s5_c_re, s5_c_im)

    for l in range(DEPTH):
        j = l // 2
        m1, m2 = mods[2 * l], mods[2 * l + 1]
        pmod = post_mod(l, m1, m2)
        if l % 2 == 0:
            qm, km, vt, qs, ks, vs = _att_in(x, in_mod(m1), j, w_in_att, q_norm, w_uq, kv_norm, w_uk, w_uv,
                                             att_tab, tm)
            o_a = _mla_attention(qm, km, vt, _pick_tile(S, 1024))
            o_b = _swa_attention(swa_sinks[j], qs, ks, vs, tm)
            x = _post(_att_post_kernel, x, [o_a, o_b], pmod,
                      [(w_out_att, j), (w1, l), (w2, l)], tm, "att_post")
        else:
            rq, rk, rv, rg, u = _rec_in(x, in_mod(m1), j, w_in_rec, rec_tab, tm)
            o_c = _retention(rq, rk, rv, rg, tm)
            ys = _s5_scan(u, [a[j] for a in s5_prep])
            x = _post(_rec_post_kernel, x, [o_c, ys, u], pmod,
                      [(s5_dd, j), (glu_w, j), (glu_b, j), (w_out_rec, j), (w1, l), (w2, l)], tm, "rec_post")
    return x
```

```python
import functools
import math

import jax
import jax.numpy as jnp
import numpy as np
from jax import lax
from jax.experimental import pallas as pl
from jax.experimental.pallas import tpu as pltpu

F32 = jnp.float32
BF16 = jnp.bfloat16

D_MODEL = 1024
DEPTH = 4
DN_ALPHA = (2.0 * DEPTH) ** 0.25
LN_EPS = 1e-5
RMS_EPS = 1e-6
ROPE_THETA = 10000.0
MLA_HEADS = 8
MLA_NOPE = 64
MLA_ROPE = 32
MLA_V = 64
MLA_Q_RANK = 384
MLA_KV_RANK = 256
SWA_HEADS = 8
SWA_KV_HEADS = 2
SWA_HD = 64
SWA_WINDOW = 128
RET_HEADS = 4
RET_DK = 128
RET_DV = 128
RET_CHUNK = 128
S5_WIDTH = 512
S5_GROUP = 16
S5_GROUPS = 32
S5_STATE = 64
S5_CHUNK = 128
D_FF = 4096
FF_CHUNK = 1024

LANES = 128
VMEM_LIMIT = 56 * 1024 * 1024
NEG_BIG = -1e30
HI = lax.Precision.HIGHEST
MLA_VX = MLA_V + 16
MLA_Q_SCALE = (MLA_NOPE + MLA_ROPE) ** -0.5 * math.log2(math.e)


def _cparams(sem):
    return pltpu.CompilerParams(dimension_semantics=sem, vmem_limit_bytes=VMEM_LIMIT)


def _const_spec(shape):
    nd = len(shape)
    return pl.BlockSpec(shape, lambda *_: (0,) * nd, pipeline_mode=pl.Buffered(1))


def _layer_spec(arr, lyr):
    return pl.BlockSpec((None,) + arr.shape[1:], lambda *_: (lyr, 0, 0), pipeline_mode=pl.Buffered(1))


def _dot(a, b):
    return jnp.dot(a, b, preferred_element_type=F32)


def _dot_nt(a, b):
    return lax.dot_general(a, b, (((1,), (1,)), ((), ())), preferred_element_type=F32)


def _dot_tn(a, b):
    return lax.dot_general(a, b, (((0,), (0,)), ((), ())), preferred_element_type=F32)


def _rope(x, c, s):
    return x * c + pltpu.roll(x, 64, 1) * s


def _layer_norm(v, g, b):
    mu = jnp.mean(v, axis=-1, keepdims=True)
    d = v - mu
    var = jnp.mean(d * d, axis=-1, keepdims=True)
    return d * lax.rsqrt(var + LN_EPS) * g + b


def _mod_kernel(c_ref, w_ref, b_ref, o_ref):
    cond = jax.nn.silu(c_ref[...]).astype(BF16)
    o_ref[0] = _dot(cond, w_ref[0].astype(BF16)) + b_ref[0]


def _modulation(c, ada_w, ada_b):
    B = c.shape[0]
    nmod = ada_w.shape[0] * ada_w.shape[1]
    rows = 8
    tn = 768
    c8 = jnp.zeros((rows, D_MODEL), F32).at[:B].set(c)
    w = ada_w.reshape(nmod, D_MODEL, 3 * D_MODEL)
    b = ada_b.reshape(nmod, 1, 3 * D_MODEL)
    out = pl.pallas_call(
        _mod_kernel,
        out_shape=jax.ShapeDtypeStruct((nmod, rows, 3 * D_MODEL), F32),
        grid=(nmod, 3 * D_MODEL // tn),
        in_specs=[
            pl.BlockSpec((rows, D_MODEL), lambda m, n: (0, 0)),
            pl.BlockSpec((1, D_MODEL, tn), lambda m, n: (m, 0, n)),
            pl.BlockSpec((1, 1, tn), lambda m, n: (m, 0, n)),
        ],
        out_specs=pl.BlockSpec((1, rows, tn), lambda m, n: (m, 0, n)),
        compiler_params=_cparams(("arbitrary", "arbitrary")),
        name="adaln_mod",
    )(c8, w, b)
    return out[:, :B, :]


def _rope_pieces(seq, dim):
    half = dim // 2
    inv = ROPE_THETA ** (-jnp.arange(0, dim, 2, dtype=F32) / dim)
    ang = jnp.arange(seq, dtype=F32)[:, None] * inv[None, :]
    cos, sin = jnp.cos(ang), jnp.sin(ang)
    pad = 64 - half
    one = [jnp.ones((seq, pad), F32)] if pad else []
    zero = [jnp.zeros((seq, pad), F32)] if pad else []
    return [cos] + one + [cos] + one + [-sin] + zero + [sin] + zero


def _place_cols(w, src):
    src = np.asarray(src)
    cols = jnp.take(w, jnp.asarray(np.maximum(src, 0)), axis=-1)
    return jnp.where(jnp.asarray(src >= 0), cols, 0.0)


def _att_layouts():
    off_kr = MLA_Q_RANK + MLA_KV_RANK
    off_sq = off_kr + MLA_ROPE
    off_sk = off_sq + SWA_HEADS * SWA_HD
    off_sv = off_sk + SWA_KV_HEADS * SWA_HD
    src = list(range(off_kr))
    grp = [-1] * LANES
    for i in range(16):
        grp[i] = off_kr + i
        grp[64 + i] = off_kr + 16 + i
    src += grp
    for base, nh in ((off_sq, SWA_HEADS), (off_sk, SWA_KV_HEADS)):
        for h in range(nh):
            grp = [-1] * LANES
            for i in range(32):
                grp[i] = base + h * SWA_HD + i
                grp[64 + i] = base + h * SWA_HD + 32 + i
            src += grp
    src += list(range(off_sv, off_sv + SWA_KV_HEADS * SWA_HD))
    uq, uk, uv = [], [], []
    for h in range(MLA_HEADS):
        qb = h * (MLA_NOPE + MLA_ROPE)
        kb = h * (MLA_NOPE + MLA_V)
        gq = [-1] * LANES
        gk = [-1] * LANES
        for i in range(16):
            gq[i] = qb + MLA_NOPE + i
            gq[64 + i] = qb + MLA_NOPE + 16 + i
        for i in range(48):
            gq[16 + i] = qb + i
            gk[16 + i] = kb + i
        for i in range(16):
            gq[80 + i] = qb + 48 + i
            gk[80 + i] = kb + 48 + i
        uq += gq
        uk += gk
        uv += list(range(kb + MLA_NOPE, kb + MLA_NOPE + MLA_V))
    return src, uq, uk, uv


def _att_in_kernel(x_ref, mod_ref, win_ref, qn_ref, wuq_ref, kvn_ref, wuk_ref, wuv_ref, tab_ref,
                   qm_ref, km_ref, vm_ref, qs_ref, ks_ref, vs_ref):
    mod = mod_ref[0]
    h = (x_ref[0] * (1.0 + mod[1:2, :]) + mod[0:1, :]).astype(BF16)
    z = _dot(h, win_ref[...])
    tab = tab_ref[...]
    t = lambda k: tab[:, k * LANES:(k + 1) * LANES]

    c_q = z[:, :MLA_Q_RANK]
    c_q = c_q * lax.rsqrt(jnp.mean(c_q * c_q, axis=-1, keepdims=True) + RMS_EPS) * qn_ref[...]
    q = _dot(c_q.astype(BF16), wuq_ref[...])
    c_kv = z[:, MLA_Q_RANK:MLA_Q_RANK + MLA_KV_RANK]
    c_kv = c_kv * lax.rsqrt(jnp.mean(c_kv * c_kv, axis=-1, keepdims=True) + RMS_EPS) * kvn_ref[...]
    c_kv = c_kv.astype(BF16)
    kup = _dot(c_kv, wuk_ref[...])
    v = _dot(c_kv, wuv_ref[...])
    off = MLA_Q_RANK + MLA_KV_RANK
    kr = _rope(z[:, off:off + LANES], t(0), t(1))
    for hd in range(MLA_HEADS):
        sl = slice(hd * LANES, (hd + 1) * LANES)
        qm_ref[0, hd] = (_rope(q[:, sl], t(0), t(1)) * MLA_Q_SCALE).astype(BF16)
        km_ref[0, hd] = (kup[:, sl] + kr).astype(BF16)
    tm = v.shape[0]
    ones_row = jnp.where(lax.broadcasted_iota(jnp.int32, (MLA_VX - MLA_V, tm), 0) == 0, 1.0, 0.0)
    for hp in range(MLA_HEADS // 2):
        vt = v[:, hp * LANES:(hp + 1) * LANES].T
        vm_ref[0, hp, 0] = jnp.concatenate(
            [vt[:MLA_V], ones_row, vt[MLA_V:], ones_row], axis=0).astype(BF16)
    off += LANES
    for hd in range(SWA_HEADS):
        qs_ref[0, hd] = (_rope(z[:, off + hd * LANES:off + (hd + 1) * LANES], t(2), t(3))
                         * SWA_HD ** -0.5).astype(BF16)
    off += SWA_HEADS * LANES
    for kv in range(SWA_KV_HEADS):
        ks_ref[0, kv] = _rope(z[:, off + kv * LANES:off + (kv + 1) * LANES], t(2), t(3)).astype(BF16)
    off += SWA_KV_HEADS * LANES
    sv = z[:, off:off + LANES]
    vs_ref[0, 0] = sv.astype(BF16)
    vs_ref[0, 1] = pltpu.roll(sv, 64, 1).astype(BF16)


def _att_in(x, mod, lyr, w_in_p, q_norm, w_uq_p, kv_norm, w_uk_p, w_uv_p, tab, tm):
    B, S, D = x.shape
    bf = lambda *s: jax.ShapeDtypeStruct(s, BF16)
    tok4 = lambda nh: pl.BlockSpec((1, nh, tm, LANES), lambda b, i: (b, 0, i, 0))
    return pl.pallas_call(
        _att_in_kernel,
        out_shape=(bf(B, MLA_HEADS, S, LANES), bf(B, MLA_HEADS, S, LANES),
                   bf(B, MLA_HEADS // 2, S // tm, 2 * MLA_VX, tm),
                   bf(B, SWA_HEADS, S, LANES), bf(B, SWA_KV_HEADS, S, LANES), bf(B, 2, S, LANES)),
        grid=(B, S // tm),
        in_specs=[
            pl.BlockSpec((1, tm, D), lambda b, i: (b, i, 0)),
            pl.BlockSpec((1, 8, D), lambda b, i: (b, 0, 0)),
            _layer_spec(w_in_p, lyr), _layer_spec(q_norm, lyr), _layer_spec(w_uq_p, lyr),
            _layer_spec(kv_norm, lyr), _layer_spec(w_uk_p, lyr), _layer_spec(w_uv_p, lyr),
            pl.BlockSpec((tm, tab.shape[1]), lambda b, i: (i, 0)),
        ],
        out_specs=(tok4(MLA_HEADS), tok4(MLA_HEADS),
                   pl.BlockSpec((1, MLA_HEADS // 2, 1, 2 * MLA_VX, tm), lambda b, i: (b, 0, i, 0, 0)),
                   tok4(SWA_HEADS), tok4(SWA_KV_HEADS), tok4(2)),
        compiler_params=_cparams(("arbitrary", "arbitrary")),
        name="att_in_proj",
    )(x, mod, w_in_p, q_norm, w_uq_p, kv_norm, w_uk_p, w_uv_p, tab)


def _mla_kernel(q_ref, k_ref, vt_ref, o_ref, st0a, st0b, st1a, st1b, p0a, p0b, p1a, p1b, acc_sc, cm_sc,
                *, t, h):
    i = pl.program_id(2)
    assert t == 4 * h
    st_sets = ((st0a, st0b), (st1a, st1b))
    p_sets = ((p0a, p0b), (p1a, p1b))

    def scores(blk, st_set, half):
        k0 = pl.multiple_of(blk * h, h)
        for hh in range(2):
            st = _dot_nt(k_ref[0, hh, pl.ds(k0, h), :], q_ref[0, hh])
            st_sets[st_set][half][hh] = st
            cm_sc[(st_set * 2 + half) * 2 + hh, 0:1, :] = jnp.max(st, axis=0, keepdims=True)

    def values(vt_blk, half, p_rd, hh):
        vt = vt_ref[0, 0, vt_blk, hh * MLA_VX:(hh + 1) * MLA_VX, half * h:(half + 1) * h]
        return _dot(vt, p_rd[hh])

    def accumulate(vt_blk, carry, p_rd):
        for hh in range(2):
            _, alpha_a, alpha_b = carry[hh]
            acc = alpha_a * acc_sc[hh] + values(vt_blk, 0, p_rd[0], hh)
            acc_sc[hh] = alpha_b * acc + values(vt_blk, 1, p_rd[1], hh)

    def softmax(carry, par, diag):
        out = []
        for hh in range(2):
            m = carry[hh][0]
            alphas = []
            for half in range(2):
                st = st_sets[par][half][hh]
                if diag is None:
                    blk_max = cm_sc[(par * 2 + half) * 2 + hh, 0:1, :]
                else:
                    key = lax.broadcasted_iota(jnp.int32, (h, t), 0) + (2 * diag + half) * h
                    qry = lax.broadcasted_iota(jnp.int32, (h, t), 1)
                    st = jnp.where(key <= qry, st, NEG_BIG)
                    blk_max = jnp.max(st, axis=0, keepdims=True)
                m_new = jnp.maximum(m, blk_max)
                alphas.append(jnp.exp2(m - m_new))
                p_sets[par][half][hh] = jnp.exp2(st - m_new).astype(BF16)
                m = m_new
            out.append((m, alphas[0], alphas[1]))
        return tuple(out)

    def iteration(c, par, carry, diag=None, more=True):
        if more:
            scores(2 * c + 2, 1 - par, 0)
            scores(2 * c + 3, 1 - par, 1)
        accumulate(jnp.maximum(c - 1, 0), carry, p_sets[1 - par])
        return softmax(carry, par, diag)

    def two_pairs(cc, carry):
        carry = iteration(2 * cc, 0, carry)
        return iteration(2 * cc + 1, 1, carry)

    scores(0, 0, 0)
    scores(1, 0, 1)
    p1a[...] = jnp.zeros(p1a.shape, BF16)
    p1b[...] = jnp.zeros(p1b.shape, BF16)
    acc_sc[...] = jnp.zeros(acc_sc.shape, F32)
    one = jnp.ones((1, t), F32)
    init = (jnp.full((1, t), NEG_BIG, F32), one, one)
    carry = lax.fori_loop(0, i, two_pairs, (init, init))
    carry = iteration(2 * i, 0, carry, diag=0)
    carry = iteration(2 * i + 1, 1, carry, diag=1, more=False)
    accumulate(2 * i + 1, carry, p_sets[1])
    outs = []
    for hh in range(2):
        acc = acc_sc[hh]
        outs.append(acc[:MLA_V] / acc[MLA_V:MLA_V + 1])
    o_ref[0] = jnp.concatenate(outs, axis=0).T.astype(BF16)


def _mla_attention(qm, km, vt, t):
    B, H, S, _ = qm.shape
    tv = vt.shape[-1]
    h = tv // 2
    assert t % tv == 0
    return pl.pallas_call(
        functools.partial(_mla_kernel, t=t, h=h),
        out_shape=jax.ShapeDtypeStruct((B, S, H * MLA_V), BF16),
        grid=(B, H // 2, S // t),
        in_specs=[
            pl.BlockSpec((1, 2, t, LANES), lambda b, hp, i: (b, hp, i, 0)),
            pl.BlockSpec((1, 2, S, LANES), lambda b, hp, i: (b, hp, 0, 0)),
            pl.BlockSpec((1, 1, S // tv, 2 * MLA_VX, tv), lambda b, hp, i: (b, hp, 0, 0, 0)),
        ],
        out_specs=pl.BlockSpec((1, t, LANES), lambda b, hp, i: (b, i, hp)),
        scratch_shapes=([pltpu.VMEM((2, h, t), F32)] * 4 + [pltpu.VMEM((2, h, t), BF16)] * 4
                        + [pltpu.VMEM((2, MLA_VX, t), F32), pltpu.VMEM((8, 8, t), F32)]),
        compiler_params=_cparams(("arbitrary", "arbitrary", "arbitrary")),
        name="mla_flash",
    )(qm, km, vt)


def _swa_kernel(sink_ref, q_ref, kc_ref, kp_ref, vc_ref, vp_ref, o_ref, s_sc, p_sc, *, tq):
    W = SWA_WINDOW
    G = SWA_HEADS // SWA_KV_HEADS
    i = pl.program_id(1)
    nsub = tq // W
    r = lax.broadcasted_iota(jnp.int32, (G * W, 2 * W), 0) % W
    c = lax.broadcasted_iota(jnp.int32, (G * W, 2 * W), 1)
    band = (c > r) & (c <= r + W)
    first = band & ((i > 0) | (c >= W))
    lane = lax.broadcasted_iota(jnp.int32, (W, LANES), 1)
    for kv in range(SWA_KV_HEADS):
        kcat = jnp.concatenate([kp_ref[0, kv], kc_ref[0, kv]], axis=0)
        for n in range(nsub):
            q4 = jnp.concatenate([q_ref[0, kv * G + g, n * W:(n + 1) * W, :] for g in range(G)], axis=0)
            s_sc[kv * nsub + n] = _dot_nt(q4, kcat[n * W:(n + 2) * W])
    for kv in range(SWA_KV_HEADS):
        sink = jnp.concatenate(
            [jnp.full((W, 1), sink_ref[kv * G + g], F32) for g in range(G)], axis=0)
        for n in range(nsub):
            s = jnp.where(first if n == 0 else band, s_sc[kv * nsub + n], NEG_BIG)
            m = jnp.maximum(jnp.max(s, axis=-1, keepdims=True), sink)
            p = jnp.exp(s - m)
            den = jnp.sum(p, axis=-1, keepdims=True) + jnp.exp(sink - m)
            p_sc[kv * nsub + n] = (p / den).astype(BF16)
    va = jnp.concatenate([vp_ref[0, 0], vc_ref[0, 0]], axis=0)
    vb = jnp.concatenate([vp_ref[0, 1], vc_ref[0, 1]], axis=0)
    for kv in range(SWA_KV_HEADS):
        v_even, v_odd = (va, vb) if kv == 0 else (vb, va)
        for n in range(nsub):
            for pr in range(G // 2):
                oe = _dot(p_sc[kv * nsub + n, (2 * pr) * W:(2 * pr + 1) * W, :], v_even[n * W:(n + 2) * W])
                oo = _dot(p_sc[kv * nsub + n, (2 * pr + 1) * W:(2 * pr + 2) * W, :], v_odd[n * W:(n + 2) * W])
                blk = kv * (G // 2) + pr
                o_ref[0, n * W:(n + 1) * W, blk * LANES:(blk + 1) * LANES] = (
                    jnp.where(lane < SWA_HD, oe, oo).astype(BF16))


def _swa_attention(sinks, qs, ks, vs, tq):
    B, _, S, _ = qs.shape
    W = SWA_WINDOW
    nb = tq // W
    cur = lambda nh: pl.BlockSpec((1, nh, tq, LANES), lambda b, i: (b, 0, i, 0))
    prev = lambda nh: pl.BlockSpec((1, nh, W, LANES), lambda b, i: (b, 0, jnp.maximum(i * nb - 1, 0), 0))
    return pl.pallas_call(
        functools.partial(_swa_kernel, tq=tq),
        out_shape=jax.ShapeDtypeStruct((B, S, SWA_HEADS * SWA_HD), BF16),
        grid=(B, S // tq),
        in_specs=[
            pl.BlockSpec(memory_space=pltpu.SMEM),
            cur(SWA_HEADS), cur(SWA_KV_HEADS), prev(SWA_KV_HEADS), cur(2), prev(2),
        ],
        out_specs=pl.BlockSpec((1, tq, SWA_HEADS * SWA_HD), lambda b, i: (b, i, 0)),
        scratch_shapes=[pltpu.VMEM((SWA_KV_HEADS * nb, SWA_HEADS // SWA_KV_HEADS * W, 2 * W), F32),
                        pltpu.VMEM((SWA_KV_HEADS * nb, SWA_HEADS // SWA_KV_HEADS * W, 2 * W), BF16)],
        compiler_params=_cparams(("arbitrary", "arbitrary")),
        name="swa_attn",
    )(sinks, qs, ks, ks, vs, vs)


def _mlp_tail(x, y, mod, w1_ref, w2_ref):
    x1 = _layer_norm(DN_ALPHA * x + (1.0 + mod[0:1, :]) * y, mod[4:5, :], mod[5:6, :])
    h = (x1 * (1.0 + mod[2:3, :]) + mod[1:2, :]).astype(BF16)
    acc = None
    for cidx in range(D_FF // FF_CHUNK):
        hid = _dot(h, w1_ref[:, cidx * FF_CHUNK:(cidx + 1) * FF_CHUNK])
        hid = jnp.square(jnp.maximum(hid, 0.0)).astype(BF16)
        part = _dot(hid, w2_ref[cidx * FF_CHUNK:(cidx + 1) * FF_CHUNK, :])
        acc = part if acc is None else acc + part
    return _layer_norm(DN_ALPHA * x1 + (1.0 + mod[3:4, :]) * acc, mod[6:7, :], mod[7:8, :])


def _att_post_kernel(x_ref, oa_ref, ob_ref, mod_ref, wout_ref, w1_ref, w2_ref, o_ref):
    half = wout_ref.shape[0] // 2
    y = _dot(oa_ref[0], wout_ref[:half, :]) + _dot(ob_ref[0], wout_ref[half:, :])
    o_ref[0] = _mlp_tail(x_ref[0], y, mod_ref[0], w1_ref, w2_ref)


def _rec_post_kernel(x_ref, oc_ref, ys_ref, u_ref, mod_ref, d_ref, gw_ref, gb_ref,
                     wout_ref, w1_ref, w2_ref, o_ref):
    half = wout_ref.shape[0] // 2
    yy = jax.nn.gelu(ys_ref[0] + d_ref[...] * u_ref[0])
    od = yy * jax.nn.sigmoid(_dot(yy.astype(BF16), gw_ref[...]) + gb_ref[...])
    y = _dot(oc_ref[0], wout_ref[:half, :]) + _dot(od.astype(BF16), wout_ref[half:, :])
    o_ref[0] = _mlp_tail(x_ref[0], y, mod_ref[0], w1_ref, w2_ref)


def _post(kernel, x, toks, mod, params, tm, name):
    B, S, D = x.shape
    tok = lambda a: pl.BlockSpec((1, tm, a.shape[-1]), lambda b, i: (b, i, 0))
    return pl.pallas_call(
        kernel,
        out_shape=jax.ShapeDtypeStruct((B, S, D), F32),
        grid=(B, S // tm),
        in_specs=([tok(x)] + [tok(a) for a in toks]
                  + [pl.BlockSpec((1, 8, D), lambda b, i: (b, 0, 0))]
                  + [_layer_spec(a, lyr) for a, lyr in params]),
        out_specs=tok(x),
        compiler_params=_cparams(("arbitrary", "arbitrary")),
        name=name,
    )(x, *toks, mod, *[a for a, _ in params])


def _rec_in_kernel(x_ref, mod_ref, win_ref, tab_ref, q_ref, k_ref, v_ref, g_ref, u_ref):
    mod = mod_ref[0]
    h = (x_ref[0] * (1.0 + mod[1:2, :]) + mod[0:1, :]).astype(BF16)
    z = _dot(h, win_ref[...])
    tab = tab_ref[...]
    t = lambda k: tab[:, k * LANES:(k + 1) * LANES]
    W = RET_HEADS * RET_DK
    for hd in range(RET_HEADS):
        sl = slice(hd * LANES, (hd + 1) * LANES)
        q_ref[0, :, sl] = _rope(z[:, hd * LANES:(hd + 1) * LANES], t(0), t(1)).astype(BF16)
        k_ref[0, :, sl] = _rope(z[:, W + hd * LANES:W + (hd + 1) * LANES], t(0), t(1)) * RET_DK ** -0.5
    v_ref[0] = z[:, 2 * W:3 * W].astype(BF16)
    g_ref[0] = z[:, 3 * W:4 * W]
    u_ref[0] = z[:, 4 * W:]


def _rec_in(x, mod, lyr, w_in, tab, tm):
    B, S, D = x.shape
    W = RET_HEADS * RET_DK
    sd = lambda dt: jax.ShapeDtypeStruct((B, S, W), dt)
    tok = pl.BlockSpec((1, tm, W), lambda b, i: (b, i, 0))
    return pl.pallas_call(
        _rec_in_kernel,
        out_shape=(sd(BF16), sd(F32), sd(BF16), sd(F32), sd(F32)),
        grid=(B, S // tm),
        in_specs=[
            pl.BlockSpec((1, tm, D), lambda b, i: (b, i, 0)),
            pl.BlockSpec((1, 8, D), lambda b, i: (b, 0, 0)),
            _layer_spec(w_in, lyr),
            pl.BlockSpec((tm, tab.shape[1]), lambda b, i: (i, 0)),
        ],
        out_specs=(tok, tok, tok, tok, tok),
        compiler_params=_cparams(("arbitrary", "arbitrary")),
        name="rec_in_proj",
    )(x, mod, w_in, tab)


def _ret_kernel(q_ref, k_ref, v_ref, g_ref, dec_ref, te_ref, fs_ref, cd_ref, o_ref,
                st_sc, sc_sc, upd_sc, sin_sc, *, tm):
    C = RET_CHUNK
    nch = tm // C

    @pl.when(pl.program_id(1) == 0)
    def _():
        st_sc[...] = jnp.zeros(st_sc.shape, F32)

    def operands(hd, n):
        sl = slice(hd * LANES, (hd + 1) * LANES)
        rows = slice(n * C, (n + 1) * C)
        return rows, sl

    for hd in range(RET_HEADS):
        for n in range(nch):
            rows, sl = operands(hd, n)
            k = k_ref[0, rows, sl]
            v = v_ref[0, rows, sl]
            sc_sc[hd * nch + n] = (_dot_nt(q_ref[0, rows, sl], k.astype(BF16)) * dec_ref[hd]).astype(BF16)
            upd_sc[hd * nch + n] = _dot_tn((k * te_ref[hd]).astype(BF16), v)
    for hd in range(RET_HEADS):
        state = st_sc[hd]
        for n in range(nch):
            sin_sc[hd * nch + n] = state.astype(BF16)
            state = cd_ref[hd] * state + upd_sc[hd * nch + n]
        st_sc[hd] = state
    for hd in range(RET_HEADS):
        for n in range(nch):
            rows, sl = operands(hd, n)
            o = (_dot(sc_sc[hd * nch + n], v_ref[0, rows, sl])
                 + _dot(q_ref[0, rows, sl], sin_sc[hd * nch + n]) * fs_ref[hd])
            mu = jnp.mean(o, axis=-1, keepdims=True)
            d = o - mu
            var = jnp.mean(d * d, axis=-1, keepdims=True)
            o = d * lax.rsqrt(var + LN_EPS)
            o_ref[0, rows, sl] = (jax.nn.silu(g_ref[0, rows, sl]) * o).astype(BF16)


def _retention(q, k, v, g, tm):
    B, S, W = q.shape
    H, C = RET_HEADS, RET_CHUNK
    log_gamma = jnp.log(1.0 - 2.0 ** (-5.0 - jnp.arange(H, dtype=F32)))
    idx = jnp.arange(C, dtype=F32)
    diff = idx[:, None] - idx[None, :]
    decay = jnp.where(diff >= 0, jnp.exp(log_gamma[:, None, None] * jnp.maximum(diff, 0.0)), 0.0)
    to_end = jnp.exp(log_gamma[:, None] * (C - 1.0 - idx)[None, :])
    from_start = jnp.exp((idx + 1.0)[None, :] * log_gamma[:, None])
    chunk_decay = jnp.exp(log_gamma * C)
    bc = lambda a: jnp.broadcast_to(a[:, :, None], (H, C, LANES))
    cd = jnp.broadcast_to(chunk_decay[:, None, None], (H, RET_DK, RET_DV))
    tok = pl.BlockSpec((1, tm, W), lambda b, i: (b, i, 0))
    tbl = _const_spec((H, C, LANES))
    return pl.pallas_call(
        functools.partial(_ret_kernel, tm=tm),
        out_shape=jax.ShapeDtypeStruct((B, S, W), BF16),
        grid=(B, S // tm),
        in_specs=[tok, tok, tok, tok, tbl, tbl, tbl, tbl],
        out_specs=tok,
        scratch_shapes=[pltpu.VMEM((H, RET_DK, RET_DV), F32),
                        pltpu.VMEM((H * tm // C, C, C), BF16),
                        pltpu.VMEM((H * tm // C, RET_DK, RET_DV), F32),
                        pltpu.VMEM((H * tm // C, RET_DK, RET_DV), BF16)],
        compiler_params=_cparams(("arbitrary", "arbitrary")),
        name="retention",
    )(q, k, v, g, decay, bc(to_end), bc(from_start), cd)


def _s5_kernel(u_ref, kern_ref, we_ref, wc_ref, mc_ref, ms_ref, y_ref, toep_sc, *, nchunk, nsteps):
    L = S5_CHUNK
    s_i = lax.broadcasted_iota(jnp.int32, (L, L), 0)
    t_i = lax.broadcasted_iota(jnp.int32, (L, L), 1)
    causal = t_i >= s_i
    for q in range(S5_GROUP):
        taps = kern_ref[0, q]
        for p in range(S5_GROUP):
            blk = pltpu.roll(jnp.broadcast_to(taps[p:p + 1, :], (L, L)), 0, 1, stride=1, stride_axis=0)
            toep_sc[q * L:(q + 1) * L, p * L:(p + 1) * L] = jnp.where(causal, blk, 0.0).astype(BF16)
    u = u_ref[0]
    R = u.shape[0]
    y = _dot(u, toep_sc[...])
    x = _dot(u, we_ref[0])
    cidx = lax.broadcasted_iota(jnp.int32, (R, LANES), 0) % nchunk
    mc = mc_ref[0]
    ms = ms_ref[0]
    for kk in range(nsteps):
        sh = 1 << kk
        xs = jnp.where(cidx >= sh, pltpu.roll(x, sh, 0), 0.0)
        x = x + xs * mc[kk:kk + 1, :] + pltpu.roll(xs, 64, 1) * ms[kk:kk + 1, :]
    xprev = jnp.where(cidx >= 1, pltpu.roll(x, 1, 0), 0.0)
    y_ref[0] = y + _dot(xprev.astype(BF16), wc_ref[0])


def _s5_prep(a_re, a_im, log_step, b_re, b_im, c_re, c_im, L, nsteps):
    G, N, P = S5_GROUPS, S5_STATE, S5_GROUP
    dt = jnp.exp(log_step)[:, None]
    lr, li = a_re, a_im
    la, th = lr * dt, li * dt
    mag = jnp.exp(la)
    ar, ai = mag * jnp.cos(th), mag * jnp.sin(th)
    den = lr * lr + li * li
    cr = ((ar - 1.0) * lr + ai * li) / den
    ci = (ai * lr - (ar - 1.0) * li) / den
    bbr = cr[..., None] * b_re - ci[..., None] * b_im
    bbi = cr[..., None] * b_im + ci[..., None] * b_re
    tau = jnp.arange(L + 1, dtype=F32)[:, None, None]
    pm = jnp.exp(la[None] * tau)
    pr, pi = pm * jnp.cos(th[None] * tau), pm * jnp.sin(th[None] * tau)
    car = c_re[None] * pr[:, :, None, :] - c_im[None] * pi[:, :, None, :]
    cai = c_re[None] * pi[:, :, None, :] + c_im[None] * pr[:, :, None, :]
    kern = (jnp.einsum('tgpn,gnq->gqpt', car[:L], bbr, precision=HI)
            - jnp.einsum('tgpn,gnq->gqpt', cai[:L], bbi, precision=HI))
    rev_r, rev_i = pr[L - 1::-1][:L], pi[L - 1::-1][:L]
    we_r = rev_r[..., None] * bbr[None] - rev_i[..., None] * bbi[None]
    we_i = rev_r[..., None] * bbi[None] + rev_i[..., None] * bbr[None]
    we = jnp.concatenate([we_r, we_i], axis=2).transpose(1, 3, 0, 2).reshape(G, P * L, 2 * N)
    wc = jnp.concatenate([car[1:], -cai[1:]], axis=3)
    wc = wc.transpose(1, 3, 2, 0).reshape(G, 2 * N, P * L)
    zr, zi = pr[L], pi[L]
    mcs, mss = [], []
    for _ in range(nsteps):
        mcs.append(jnp.concatenate([zr, zr], axis=-1))
        mss.append(jnp.concatenate([-zi, zi], axis=-1))
        zr, zi = zr * zr - zi * zi, 2.0 * zr * zi
    mc = jnp.stack(mcs, axis=1)
    ms = jnp.stack(mss, axis=1)
    return kern, we.astype(BF16), wc.astype(BF16), mc, ms


def _s5_scan(u, a_re, a_im, log_step, b_re, b_im, c_re, c_im):
    B, S, W = u.shape
    G, P, N, L = S5_GROUPS, S5_GROUP, S5_STATE, S5_CHUNK
    nchunk = S // L
    nsteps = max(1, (nchunk - 1).bit_length())
    kern, we, wc, mc, ms = _s5_prep(a_re, a_im, log_step, b_re, b_im, c_re, c_im, L, nsteps)
    R = B * nchunk
    ug = u.astype(BF16).reshape(B, nchunk, L, G, P).transpose(3, 0, 1, 4, 2).reshape(G, R, P * L)
    grp = lambda a: pl.BlockSpec((1,) + a.shape[1:], lambda g: (g,) + (0,) * (a.ndim - 1))
    y = pl.pallas_call(
        functools.partial(_s5_kernel, nchunk=nchunk, nsteps=nsteps),
        out_shape=jax.ShapeDtypeStruct((G, R, P * L), F32),
        grid=(G,),
        in_specs=[grp(ug), grp(kern), grp(we), grp(wc), grp(mc), grp(ms)],
        out_specs=pl.BlockSpec((1, R, P * L), lambda g: (g, 0, 0)),
        scratch_shapes=[pltpu.VMEM((P * L, P * L), BF16)],
        compiler_params=_cparams(("arbitrary",)),
        name="s5_conv",
    )(ug, kern, we, wc, mc, ms)
    return y.reshape(G, B, nchunk, P, L).transpose(1, 2, 4, 0, 3).reshape(B, S, W)


def _pick_tile(S, pref):
    t = min(pref, S)
    assert S % t == 0, (S, t)
    return t


def kernel(x, c, ada_w, ada_b, ln_g, ln_b, att_w_in, mla_q_norm, mla_w_uq, mla_kv_norm, mla_w_ukv, swa_sinks, att_w_out, rec_w_in, s5_a_re, s5_a_im, s5_log_step, s5_b_re, s5_b_im, s5_c_re, s5_c_im, s5_d, s5_glu_w, s5_glu_b, rec_w_out, mlp_w1, mlp_w2):
    B, S, D = x.shape
    assert D == D_MODEL and S % SWA_WINDOW == 0 and S % S5_CHUNK == 0
    tm = _pick_tile(S, 512)

    mods = _modulation(c, ada_w, ada_b)
    zeros = jnp.zeros((B, D), F32)

    def in_mod(m):
        rows = [m[:, :D], m[:, D:2 * D]] + [zeros] * 6
        return jnp.stack(rows, axis=1)

    def post_mod(l, m1, m2):
        bc = lambda v: jnp.broadcast_to(v[None, :], (B, D))
        rows = [m1[:, 2 * D:], m2[:, :D], m2[:, D:2 * D], m2[:, 2 * D:],
                bc(ln_g[l, 0]), bc(ln_b[l, 0]), bc(ln_g[l, 1]), bc(ln_b[l, 1])]
        return jnp.stack(rows, axis=1)

    src_in, src_uq, src_uk, src_uv = _att_layouts()
    att_tab = jnp.concatenate(_rope_pieces(S, MLA_ROPE) + _rope_pieces(S, SWA_HD), axis=1)
    rec_tab = jnp.concatenate(_rope_pieces(S, RET_DK), axis=1)

    w_in_att = _place_cols(att_w_in, src_in).astype(BF16)
    w_uq = _place_cols(mla_w_uq, src_uq).astype(BF16)
    w_uk = _place_cols(mla_w_ukv, src_uk).astype(BF16)
    w_uv = _place_cols(mla_w_ukv, src_uv).astype(BF16)
    q_norm = mla_q_norm[:, None, :]
    kv_norm = mla_kv_norm[:, None, :]
    w_out_att = att_w_out.astype(BF16)
    w_in_rec = rec_w_in.astype(BF16)
    w_out_rec = rec_w_out.astype(BF16)
    glu_w = s5_glu_w.astype(BF16)
    s5_dd = s5_d[:, None, :]
    glu_b = s5_glu_b[:, None, :]
    w1 = mlp_w1.astype(BF16)
    w2 = mlp_w2.astype(BF16)

    for l in range(DEPTH):
        j = l // 2
        m1, m2 = mods[2 * l], mods[2 * l + 1]
        pmod = post_mod(l, m1, m2)
        if l % 2 == 0:
            qm, km, vt, qs, ks, vs = _att_in(x, in_mod(m1), j, w_in_att, q_norm, w_uq, kv_norm, w_uk, w_uv,
                                             att_tab, tm)
            o_a = _mla_attention(qm, km, vt, _pick_tile(S, 1024))
            o_b = _swa_attention(swa_sinks[j], qs, ks, vs, _pick_tile(S, 2048))
            x = _post(_att_post_kernel, x, [o_a, o_b], pmod,
                      [(w_out_att, j), (w1, l), (w2, l)], tm, "att_post")
        else:
            rq, rk, rv, rg, u = _rec_in(x, in_mod(m1), j, w_in_rec, rec_tab, tm)
            o_c = _retention(rq, rk, rv, rg, tm)
            ys = _s5_scan(u, s5_a_re[j], s5_a_im[j], s5_log_step[j], s5_b_re[j], s5_b_im[j],
                          s5_c_re[j], s5_c_im[j])
            x = _post(_rec_post_kernel, x, [o_c, ys, u], pmod,
                      [(s5_dd, j), (glu_w, j), (glu_b, j), (w_out_rec, j), (w1, l), (w2, l)], tm, "rec_post")
    return x
```
